```python
import jax, jax.numpy as jnp
from jax import lax
import numpy as np

D_MODEL = 1024
BATCH = 4
SEQ = 4096
DEPTH = 1

HEAD_DIM = 64
SWA_Q_HEADS = 8
SWA_KV_HEADS = 2
SWA_GROUP = SWA_Q_HEADS // SWA_KV_HEADS
WINDOW = 128
BLOCK = 128
FOX_HEADS = 8
SWA_WIDTH = SWA_Q_HEADS * HEAD_DIM
SWA_KV_WIDTH = SWA_KV_HEADS * HEAD_DIM
FOX_WIDTH = FOX_HEADS * HEAD_DIM
NUM_BUCKETS = 32
MAX_DISTANCE = 128
PEER_HEADS = 8
N_KEYS = 128
N_EXPERTS = N_KEYS * N_KEYS
PEER_TOPK = 16
PEER_QDIM = 128
PEER_HALF = PEER_QDIM // 2
PEER_CHUNK = 128
N_MOD = 6
EPS = 1e-6
NEG = -1e30
IN_SPLITS = (SWA_WIDTH, SWA_KV_WIDTH, SWA_KV_WIDTH, FOX_WIDTH, FOX_WIDTH, FOX_WIDTH, FOX_HEADS, D_MODEL, D_MODEL)
IN_WIDTH = sum(IN_SPLITS)
IN_OFFSETS = tuple(int(o) for o in np.cumsum(IN_SPLITS)[:-1])

kernel_name = "gated_swa_fox_peer_hybrid"


def _rmsnorm(x, g):
    xf = x.astype(jnp.float32)
    xf = xf * lax.rsqrt(jnp.mean(xf * xf, axis=-1, keepdims=True) + EPS)
    return xf.astype(x.dtype) * g


def _modulate(h, shift, scale):
    return h * (1 + scale[:, None, :]) + shift[:, None, :]


def _t5_bucket(dist):
    dist = np.clip(dist, 0, None)
    max_exact = NUM_BUCKETS // 2
    large = max_exact + (np.log(np.maximum(dist, 1) / max_exact) / np.log(MAX_DISTANCE / max_exact)
                         * (NUM_BUCKETS - max_exact)).astype(np.int32)
    large = np.minimum(large, NUM_BUCKETS - 1)
    return np.where(dist < max_exact, dist, large).astype(np.int32)


def _swa_attention(q, k, v, sinks, rel_bias):
    B, S = q.shape[0], q.shape[1]
    nb = S // BLOCK
    q = q.reshape(B, nb, BLOCK, SWA_KV_HEADS, SWA_GROUP, HEAD_DIM)
    k = k.reshape(B, S, SWA_KV_HEADS, HEAD_DIM)
    v = v.reshape(B, S, SWA_KV_HEADS, HEAD_DIM)

    def band(t):
        tp = jnp.pad(t, ((0, 0), (BLOCK, 0), (0, 0), (0, 0)))
        prev = tp[:, :S].reshape(B, nb, BLOCK, SWA_KV_HEADS, HEAD_DIM)
        return jnp.concatenate([prev, t.reshape(B, nb, BLOCK, SWA_KV_HEADS, HEAD_DIM)], axis=2)

    kb, vb = band(k), band(v)
    logits = jnp.einsum('bnqhgd,bnkhd->bnhgqk', q, kb).astype(jnp.float32) * (HEAD_DIM ** -0.5)
    qi = np.arange(BLOCK)[:, None]
    kj = np.arange(2 * BLOCK)[None, :]
    dist = qi + BLOCK - kj
    in_band = (dist >= 0) & (dist < WINDOW)
    key_exists = (np.arange(nb)[:, None, None] * BLOCK + kj[None] - BLOCK) >= 0
    valid = in_band[None] & key_exists
    bias = rel_bias[_t5_bucket(dist)].astype(jnp.float32)
    bias = jnp.transpose(bias, (2, 0, 1)).reshape(SWA_KV_HEADS, SWA_GROUP, BLOCK, 2 * BLOCK)
    logits = logits + bias[None, None]
    logits = jnp.where(valid[None, :, None, None], logits, NEG)
    sink = jnp.broadcast_to(sinks.astype(jnp.float32).reshape(SWA_KV_HEADS, SWA_GROUP, 1, 1),
                            logits.shape[:-1] + (1,))
    p = jax.nn.softmax(jnp.concatenate([logits, sink], axis=-1), axis=-1)[..., :-1]
    o = jnp.einsum('bnhgqk,bnkhd->bnqhgd', p.astype(v.dtype), vb)
    return o.reshape(B, S, SWA_WIDTH)


def _fox_attention(q, k, v, log_f):
    B, S = q.shape[0], q.shape[1]
    nb = S // BLOCK
    q = q.reshape(B, S, FOX_HEADS, HEAD_DIM)
    k = k.reshape(B, S, FOX_HEADS, HEAD_DIM)
    v = v.reshape(B, S, FOX_HEADS, HEAD_DIM)
    F = jnp.transpose(jnp.cumsum(log_f, axis=1), (0, 2, 1))
    qb = jnp.transpose(q.reshape(B, nb, BLOCK, FOX_HEADS, HEAD_DIM), (1, 0, 2, 3, 4))
    Fb = jnp.transpose(F.reshape(B, FOX_HEADS, nb, BLOCK), (2, 0, 1, 3))
    kpos = jnp.arange(S)

    def block(args):
        qi, Fi, n = args
        s = jnp.einsum('bqhd,bkhd->bhqk', qi, k).astype(jnp.float32) * (HEAD_DIM ** -0.5)
        s = s + Fi[..., :, None] - F[:, :, None, :]
        qpos = n * BLOCK + jnp.arange(BLOCK)
        s = jnp.where((kpos[None, :] <= qpos[:, None])[None, None], s, NEG)
        p = jax.nn.softmax(s, axis=-1)
        return jnp.einsum('bhqk,bkhd->bqhd', p.astype(v.dtype), v)

    out = lax.map(block, (qb, Fb, jnp.arange(nb)))
    return jnp.transpose(out, (1, 0, 2, 3, 4)).reshape(B, S, FOX_WIDTH)


def _peer(h, w_query, sub_keys, expert_down, expert_up):
    B, S, D = h.shape
    ht = h.reshape(-1, PEER_CHUNK, D)

    def chunk(xc):
        T = xc.shape[0]
        q = (xc @ w_query).reshape(T, PEER_HEADS, 2, PEER_HALF)
        s = jnp.einsum('thpd,hpkd->thpk', q, sub_keys).astype(jnp.float32)
        s1, i1 = lax.top_k(s[:, :, 0], PEER_TOPK)
        s2, i2 = lax.top_k(s[:, :, 1], PEER_TOPK)
        cand = (s1[..., :, None] + s2[..., None, :]).reshape(T, PEER_HEADS, PEER_TOPK * PEER_TOPK)
        cidx = (i1[..., :, None] * N_KEYS + i2[..., None, :]).reshape(T, PEER_HEADS, PEER_TOPK * PEER_TOPK)
        top, pos = lax.top_k(cand, PEER_TOPK)
        idx = jnp.take_along_axis(cidx, pos, axis=-1)
        g = jax.nn.softmax(top, axis=-1).astype(xc.dtype)
        u = expert_down[idx]
        act = jax.nn.gelu(jnp.einsum('td,thkd->thk', xc, u), approximate=False)
        vv = expert_up[idx]
        return jnp.einsum('thk,thkd->td', g * act, vv)

    return lax.map(chunk, ht).reshape(B, S, D)


def setup_inputs(seed: int = 0) -> dict:
    key = jax.random.key(seed)
    ks = jax.random.split(key, 20)
    f32 = jnp.float32
    nrm = lambda k, shape, s: jax.random.normal(k, shape, f32) * s
    return {
        "x": nrm(ks[0], (BATCH, SEQ, D_MODEL), 1.0),
        "c": nrm(ks[1], (BATCH, D_MODEL), 1.0),
        "norm_mix_g": 1.0 + nrm(ks[2], (DEPTH, D_MODEL), 0.02),
        "norm_ffn_g": 1.0 + nrm(ks[3], (DEPTH, D_MODEL), 0.02),
        "w_ada": nrm(ks[4], (DEPTH, D_MODEL, N_MOD * D_MODEL), 0.5 * D_MODEL ** -0.5),
        "b_ada": nrm(ks[5], (DEPTH, N_MOD * D_MODEL), 0.02),
        "w_in": nrm(ks[6], (DEPTH, D_MODEL, IN_WIDTH), D_MODEL ** -0.5),
        "b_forget": 2.0 + nrm(ks[7], (DEPTH, FOX_HEADS), 0.5),
        "sinks": nrm(ks[8], (DEPTH, SWA_Q_HEADS), 0.5),
        "w_branch_a": nrm(ks[9], (DEPTH, SWA_WIDTH, D_MODEL), SWA_WIDTH ** -0.5),
        "w_branch_b": nrm(ks[10], (DEPTH, FOX_WIDTH, D_MODEL), FOX_WIDTH ** -0.5),
        "w_out": nrm(ks[11], (DEPTH, D_MODEL, D_MODEL), D_MODEL ** -0.5),
        "rel_bias": nrm(ks[12], (NUM_BUCKETS, SWA_Q_HEADS), 0.5),
        "w_query": nrm(ks[13], (DEPTH, D_MODEL, PEER_HEADS * PEER_QDIM), D_MODEL ** -0.5),
        "sub_keys": nrm(ks[14], (DEPTH, PEER_HEADS, 2, N_KEYS, PEER_HALF), PEER_HALF ** -0.5),
        "expert_down": nrm(ks[15], (DEPTH, N_EXPERTS, D_MODEL), D_MODEL ** -0.5),
        "expert_up": nrm(ks[16], (DEPTH, N_EXPERTS, D_MODEL), 0.3),
        "final_norm_g": 1.0 + nrm(ks[17], (D_MODEL,), 0.02),
    }


def reference(x, c, norm_mix_g, norm_ffn_g, w_ada, b_ada, w_in, b_forget, sinks, w_branch_a,
              w_branch_b, w_out, rel_bias, w_query, sub_keys, expert_down, expert_up, final_norm_g):
    c_act = jax.nn.silu(c)
    for l in range(DEPTH):
        mod = c_act @ w_ada[l] + b_ada[l]
        sh1, sc1, gt1, sh2, sc2, gt2 = jnp.split(mod, N_MOD, axis=-1)

        h = _modulate(_rmsnorm(x, norm_mix_g[l]), sh1, sc1)
        proj = h @ w_in[l]
        qa, ka, va, qb, kb, vb, fl, gla, glb = jnp.split(proj, IN_OFFSETS, axis=-1)
        log_f = jax.nn.log_sigmoid((fl + b_forget[l]).astype(jnp.float32))
        oa = _swa_attention(qa, ka, va, sinks[l], rel_bias)
        ob = _fox_attention(qb, kb, vb, log_f)
        mixed = jax.nn.sigmoid(gla) * (oa @ w_branch_a[l]) + jax.nn.sigmoid(glb) * (ob @ w_branch_b[l])
        x = x + gt1[:, None, :] * (mixed @ w_out[l])

        h2 = _modulate(_rmsnorm(x, norm_ffn_g[l]), sh2, sc2)
        x = x + gt2[:, None, :] * _peer(h2, w_query[l], sub_keys[l], expert_down[l], expert_up[l])
    return _rmsnorm(x, final_norm_g)
```

```python
import functools

import numpy as np
import jax
import jax.numpy as jnp
from jax import lax
from jax.experimental import pallas as pl
from jax.experimental.pallas import tpu as pltpu

HEAD_DIM = 64
SWA_Q_HEADS = 8
SWA_KV_HEADS = 2
SWA_GROUP = SWA_Q_HEADS // SWA_KV_HEADS
WINDOW = 128
BLOCK = 128
FOX_HEADS = 8
SWA_WIDTH = SWA_Q_HEADS * HEAD_DIM
SWA_KV_WIDTH = SWA_KV_HEADS * HEAD_DIM
FOX_WIDTH = FOX_HEADS * HEAD_DIM
NUM_BUCKETS = 32
MAX_DISTANCE = 128
PEER_HEADS = 8
PEER_TOPK = 16
PEER_HALF = 64
N_MOD = 6
EPS = 1e-6
NEG = -1e30
LANES = 128
VMEM_LIMIT = 56 * 1024 * 1024

F32 = jnp.float32
BF16 = jnp.bfloat16


def _cparams(sem):
    return pltpu.CompilerParams(dimension_semantics=sem, vmem_limit_bytes=VMEM_LIMIT)


def _dot(a, b):
    return jnp.dot(a, b, preferred_element_type=F32)


def _dot_nt(a, b):
    return lax.dot_general(a, b, (((1,), (1,)), ((), ())), preferred_element_type=F32)


def _rms_mod(x, g, sc, sh):
    ms = jnp.mean(x * x, axis=-1, keepdims=True)
    return (x * lax.rsqrt(ms + EPS) * g) * (1.0 + sc) + sh


def _ada_kernel(c_ref, w_ref, b_ref, o_ref):
    c = c_ref[...]
    ca = c * jax.nn.sigmoid(c)
    o_ref[...] = _dot(ca, w_ref[...]) + b_ref[...]


def _ada(c, w_ada, b_ada):
    bsz, d = c.shape
    n = w_ada.shape[1]
    tn = d
    return pl.pallas_call(
        _ada_kernel,
        grid=(n // tn,),
        in_specs=[pl.BlockSpec((bsz, d), lambda j: (0, 0)),
                  pl.BlockSpec((d, tn), lambda j: (0, j)),
                  pl.BlockSpec((1, tn), lambda j: (0, j))],
        out_specs=pl.BlockSpec((bsz, tn), lambda j: (0, j)),
        out_shape=jax.ShapeDtypeStruct((bsz, n), F32),
        compiler_params=_cparams(("arbitrary",)),
        name="ada",
    )(c, w_ada, b_ada.reshape(1, n))


def _inproj_kernel(x_ref, g_ref, sc_ref, sh_ref, wa_ref, wb_ref, wf_ref, wg_ref,
                   qa_ref, ka_ref, va_ref, qb_ref, kb_ref, vb_ref, fl_ref, ga_ref, gb_ref):
    h = _rms_mod(x_ref[...], g_ref[...], sc_ref[0], sh_ref[0])
    hb = h.astype(BF16)
    scale = HEAD_DIM ** -0.5
    pa = _dot(hb, wa_ref[...])
    qa_ref[...] = (pa[:, :SWA_WIDTH] * scale).astype(BF16)
    ka_ref[...] = pa[:, SWA_WIDTH:SWA_WIDTH + SWA_KV_WIDTH].astype(BF16)
    va_ref[...] = pa[:, SWA_WIDTH + SWA_KV_WIDTH:].astype(BF16)
    pb = _dot(hb, wb_ref[...])
    qb_ref[...] = (pb[:, :FOX_WIDTH] * scale).astype(BF16)
    kb_ref[...] = pb[:, FOX_WIDTH:2 * FOX_WIDTH].astype(BF16)
    vb_ref[...] = pb[:, 2 * FOX_WIDTH:].astype(BF16)
    fl_ref[...] = _dot(h, wf_ref[...])
    d = ga_ref.shape[1]
    pg = _dot(hb, wg_ref[...])
    ga_ref[...] = jax.nn.sigmoid(pg[:, :d]).astype(BF16)
    gb_ref[...] = jax.nn.sigmoid(pg[:, d:]).astype(BF16)


def _inproj(x2d, g, sc1, sh1, wa, wb, wf, wg, seq, tm):
    t, d = x2d.shape
    per_b = seq // tm
    row = lambda w: pl.BlockSpec((tm, w), lambda i: (i, 0))
    full = lambda a: pl.BlockSpec(a.shape, lambda i: (0,) * a.ndim)
    mod = pl.BlockSpec((1, 1, d), lambda i: (i // per_b, 0, 0))
    widths = (SWA_WIDTH, SWA_KV_WIDTH, SWA_KV_WIDTH, FOX_WIDTH, FOX_WIDTH, FOX_WIDTH)
    out_shape = [jax.ShapeDtypeStruct((t, w), BF16) for w in widths]
    out_shape += [jax.ShapeDtypeStruct((t, LANES), F32),
                  jax.ShapeDtypeStruct((t, d), BF16), jax.ShapeDtypeStruct((t, d), BF16)]
    out_specs = [row(w) for w in widths] + [row(LANES), row(d), row(d)]
    return pl.pallas_call(
        _inproj_kernel,
        grid=(t // tm,),
        in_specs=[row(d), full(g), mod, mod, full(wa), full(wb), full(wf), full(wg)],
        out_specs=out_specs,
        out_shape=out_shape,
        compiler_params=_cparams(("parallel",)),
        name="inproj",
    )(x2d, g, sc1, sh1, wa, wb, wf, wg)


def _cumsum_kernel(fl_ref, b_ref, fcol_ref, frow_ref):
    z = fl_ref[0] + b_ref[...]
    lf = jnp.minimum(z, 0.0) - jnp.log1p(jnp.exp(-jnp.abs(z)))
    s = lf.shape[0]
    rows = lax.broadcasted_iota(jnp.int32, lf.shape, 0)
    acc = lf
    k = 1
    while k < s:
        shifted = pltpu.roll(acc, k, 0)
        acc = acc + jnp.where(rows >= k, shifted, 0.0)
        k *= 2
    fcol_ref[0] = acc
    frow_ref[0] = acc.T[:FOX_HEADS, :]


def _cumsum(fl3, bpad):
    bsz, s, _ = fl3.shape
    return pl.pallas_call(
        _cumsum_kernel,
        grid=(bsz,),
        in_specs=[pl.BlockSpec((1, s, LANES), lambda b: (b, 0, 0)),
                  pl.BlockSpec((1, LANES), lambda b: (0, 0))],
        out_specs=[pl.BlockSpec((1, s, LANES), lambda b: (b, 0, 0)),
                   pl.BlockSpec((1, FOX_HEADS, s), lambda b: (b, 0, 0))],
        out_shape=[jax.ShapeDtypeStruct((bsz, s, LANES), F32),
                   jax.ShapeDtypeStruct((bsz, FOX_HEADS, s), F32)],
        compiler_params=_cparams(("parallel",)),
        name="cumsum",
    )(fl3, bpad)


def _swa_kernel(sink_ref, q_ref, kp_ref, kc_ref, vp_ref, vc_ref, bias_ref, o_ref):
    n = pl.program_id(1)
    qi = lax.broadcasted_iota(jnp.int32, (BLOCK, 2 * BLOCK), 0)
    kj = lax.broadcasted_iota(jnp.int32, (BLOCK, 2 * BLOCK), 1)
    dist = qi + BLOCK - kj
    valid = (dist >= 0) & (dist < WINDOW) & ((kj >= BLOCK) | (n > 0))
    q = q_ref[...]
    kband = jnp.concatenate([kp_ref[...], kc_ref[...]], axis=0)
    vband = jnp.concatenate([vp_ref[...], vc_ref[...]], axis=0)
    outs = []
    for h in range(SWA_Q_HEADS):
        kh = h // SWA_GROUP
        qh = q[:, h * HEAD_DIM:(h + 1) * HEAD_DIM]
        kk = kband[:, kh * HEAD_DIM:(kh + 1) * HEAD_DIM]
        vv = vband[:, kh * HEAD_DIM:(kh + 1) * HEAD_DIM]
        s = _dot_nt(qh, kk) + bias_ref[h]
        s = jnp.where(valid, s, NEG)
        sink = sink_ref[h]
        m = jnp.maximum(jnp.max(s, axis=-1, keepdims=True), sink)
        p = jnp.exp(s - m)
        denom = jnp.sum(p, axis=-1, keepdims=True) + jnp.exp(sink - m)
        o = _dot(p.astype(BF16), vv)
        outs.append(o / denom)
    o_ref[...] = jnp.concatenate(outs, axis=-1).astype(o_ref.dtype)


def _swa(qa, ka, va, sinks, bias, bsz, seq):
    nb = seq // BLOCK
    cur = lambda b, n: (b * nb + n, 0)
    prev = lambda b, n: (b * nb + jnp.maximum(n - 1, 0), 0)
    return pl.pallas_call(
        _swa_kernel,
        grid=(bsz, nb),
        in_specs=[pl.BlockSpec(memory_space=pltpu.SMEM),
                  pl.BlockSpec((BLOCK, SWA_WIDTH), cur),
                  pl.BlockSpec((BLOCK, SWA_KV_WIDTH), prev),
                  pl.BlockSpec((BLOCK, SWA_KV_WIDTH), cur),
                  pl.BlockSpec((BLOCK, SWA_KV_WIDTH), prev),
                  pl.BlockSpec((BLOCK, SWA_KV_WIDTH), cur),
                  pl.BlockSpec(bias.shape, lambda b, n: (0, 0, 0))],
        out_specs=pl.BlockSpec((BLOCK, SWA_WIDTH), cur),
        out_shape=jax.ShapeDtypeStruct(qa.shape, BF16),
        compiler_params=_cparams(("parallel", "parallel")),
        name="swa",
    )(sinks, qa, ka, ka, va, va, bias)


def _fox_kernel(q_ref, k_ref, v_ref, fq_ref, fk_ref, o_ref, m_ref, l_ref, acc_ref, *, tq, tk):
    qi = pl.program_id(1)
    kj = pl.program_id(2)

    @pl.when(kj == 0)
    def _():
        m_ref[...] = jnp.full(m_ref.shape, NEG, F32)
        l_ref[...] = jnp.zeros(l_ref.shape, F32)
        acc_ref[...] = jnp.zeros(acc_ref.shape, F32)

    @pl.when(kj <= qi)
    def _():
        qpos = qi * tq + lax.broadcasted_iota(jnp.int32, (tq, tk), 0)
        kpos = kj * tk + lax.broadcasted_iota(jnp.int32, (tq, tk), 1)
        causal = kpos <= qpos
        q = q_ref[...]
        k = k_ref[...]
        v = v_ref[...]
        fq = fq_ref[0]
        fk = fk_ref[0]
        for h in range(FOX_HEADS):
            sl = slice(h * HEAD_DIM, (h + 1) * HEAD_DIM)
            s = _dot_nt(q[:, sl], k[:, sl]) + fq[:, h:h + 1] - fk[h:h + 1, :]
            s = jnp.where(causal, s, NEG)
            m_old = m_ref[h]
            m_new = jnp.maximum(m_old, jnp.max(s, axis=-1, keepdims=True))
            alpha = jnp.exp(m_old - m_new)
            p = jnp.exp(s - m_new)
            l_ref[h] = alpha * l_ref[h] + jnp.sum(p, axis=-1, keepdims=True)
            m_ref[h] = m_new
            acc_ref[h] = alpha * acc_ref[h] + _dot(p.astype(BF16), v[:, sl])

    @pl.when(kj == qi)
    def _():
        outs = [acc_ref[h] / l_ref[h] for h in range(FOX_HEADS)]
        o_ref[...] = jnp.concatenate(outs, axis=-1).astype(o_ref.dtype)


def _fox(qb, kb, vb, fcol, frow, bsz, seq, tq):
    tk = tq
    nq = seq // tq
    qmap = lambda b, i, j: (b * nq + i, 0)
    kmap = lambda b, i, j: (b * nq + jnp.minimum(j, i), 0)
    return pl.pallas_call(
        functools.partial(_fox_kernel, tq=tq, tk=tk),
        grid=(bsz, nq, nq),
        in_specs=[pl.BlockSpec((tq, FOX_WIDTH), qmap),
                  pl.BlockSpec((tk, FOX_WIDTH), kmap),
                  pl.BlockSpec((tk, FOX_WIDTH), kmap),
                  pl.BlockSpec((1, tq, LANES), lambda b, i, j: (b, i, 0)),
                  pl.BlockSpec((1, FOX_HEADS, tk), lambda b, i, j: (b, 0, jnp.minimum(j, i)))],
        out_specs=pl.BlockSpec((tq, FOX_WIDTH), qmap),
        out_shape=jax.ShapeDtypeStruct(qb.shape, BF16),
        scratch_shapes=[pltpu.VMEM((FOX_HEADS, tq, 1), F32),
                        pltpu.VMEM((FOX_HEADS, tq, 1), F32),
                        pltpu.VMEM((FOX_HEADS, tq, HEAD_DIM), F32)],
        compiler_params=_cparams(("parallel", "parallel", "arbitrary")),
        name="fox",
    )(qb, kb, vb, fcol, frow)


def _mix_kernel(x_ref, oa_ref, ob_ref, ga_ref, gb_ref, gt1_ref, g2_ref, sc2_ref, sh2_ref,
                wa_ref, wb_ref, wo_ref, wq_ref, x1_ref, h2t_ref, q_ref):
    ya = _dot(oa_ref[...], wa_ref[...])
    yb = _dot(ob_ref[...], wb_ref[...])
    mixed = ga_ref[...].astype(F32) * ya + gb_ref[...].astype(F32) * yb
    x1 = x_ref[...] + gt1_ref[0] * _dot(mixed.astype(BF16), wo_ref[...])
    x1_ref[...] = x1
    h2 = _rms_mod(x1, g2_ref[...], sc2_ref[0], sh2_ref[0])
    h2t_ref[...] = h2.T.astype(BF16)
    q_ref[...] = _dot(h2.astype(BF16), wq_ref[...])


def _mix(x2d, oa, ob, ga, gb, gt1, g2, sc2, sh2, wa, wb, wo, wq, seq, tm):
    t, d = x2d.shape
    per_b = seq // tm
    row = lambda w: pl.BlockSpec((tm, w), lambda i: (i, 0))
    full = lambda a: pl.BlockSpec(a.shape, lambda i: (0,) * a.ndim)
    mod = pl.BlockSpec((1, 1, d), lambda i: (i // per_b, 0, 0))
    return pl.pallas_call(
        _mix_kernel,
        grid=(t // tm,),
        in_specs=[row(d), row(SWA_WIDTH), row(FOX_WIDTH), row(d), row(d), mod, full(g2), mod, mod,
                  full(wa), full(wb), full(wo), full(wq)],
        out_specs=[row(d), pl.BlockSpec((d, tm), lambda i: (0, i)), row(wq.shape[1])],
        out_shape=[jax.ShapeDtypeStruct((t, d), F32), jax.ShapeDtypeStruct((d, t), BF16),
                   jax.ShapeDtypeStruct((t, wq.shape[1]), F32)],
        compiler_params=_cparams(("parallel",)),
        name="mix",
    )(x2d, oa, ob, ga, gb, gt1, g2, sc2, sh2, wa, wb, wo, wq)


def _top_desc(vals, count):
    tops = []
    work = vals
    for r in range(count):
        mx = jnp.max(work, axis=0, keepdims=True)
        tops.append(mx)
        if r + 1 < count:
            work = jnp.where(work == mx, -jnp.inf, work)
    return tops


def _route_kernel(q_ref, sk_ref, s2_ref, e2_ref, u1_ref, e1_ref):
    q = q_ref[...]
    for h in range(PEER_HEADS):
        q1 = q[:, (2 * h) * PEER_HALF:(2 * h + 1) * PEER_HALF]
        q2 = q[:, (2 * h + 1) * PEER_HALF:(2 * h + 2) * PEER_HALF]
        s1 = _dot_nt(sk_ref[h, 0], q1)
        s2 = _dot_nt(sk_ref[h, 1], q2)
        a = _top_desc(s1, PEER_TOPK + 1)
        b = _top_desc(s2, PEER_TOPK + 1)
        cands = [a[i] + b[j] for i in range(PEER_TOPK + 1) for j in range(PEER_TOPK + 1)
                 if (i + 1) * (j + 1) <= PEER_TOPK + 1]
        pad = (-len(cands)) % 8
        cands += [jnp.full_like(cands[0], -jnp.inf)] * pad
        c = _top_desc(jnp.concatenate(cands, axis=0), PEER_TOPK + 1)
        theta = 0.5 * (c[PEER_TOPK - 1] + c[PEER_TOPK])
        m = c[0]
        z = c[0] * 0.0
        for r in range(PEER_TOPK):
            z = z + jnp.exp(c[r] - m)
        s2_ref[h] = s2
        e2_ref[h] = jnp.exp(s2 - b[0])
        u1_ref[h] = theta - s1
        e1_ref[h] = jnp.exp(s1 - a[0]) / z


def _route(q, sub_keys, tr):
    t, w = q.shape
    nk = sub_keys.shape[2]
    spec = pl.BlockSpec((PEER_HEADS, nk, tr), lambda i: (0, 0, i))
    shp = jax.ShapeDtypeStruct((PEER_HEADS, nk, t), F32)
    return pl.pallas_call(
        _route_kernel,
        grid=(t // tr,),
        in_specs=[pl.BlockSpec((tr, w), lambda i: (i, 0)),
                  pl.BlockSpec(sub_keys.shape, lambda i: (0, 0, 0, 0))],
        out_specs=[spec] * 4,
        out_shape=[shp] * 4,
        compiler_params=_cparams(("parallel",)),
        name="route",
    )(q, sub_keys)


def _gelu(x):
    return 0.5 * x * (1.0 + lax.erf(x * (2.0 ** -0.5)))


def _experts_kernel(h2t_ref, dn_ref, upt_ref, s2_ref, e2_ref, u1_ref, e1_ref, o_ref,
                    act_ref, g_ref, *, nk, ni, tq):
    e = pl.program_id(1)

    @pl.when(e == 0)
    def _():
        o_ref[...] = jnp.zeros(o_ref.shape, F32)

    act_ref[...] = _dot(dn_ref[...], h2t_ref[...])
    for il in range(ni):
        rows = slice(il * nk, (il + 1) * nk)
        for c in range(tq // LANES):
            cols = slice(c * LANES, (c + 1) * LANES)
            w = jnp.zeros((nk, LANES), F32)
            for h in range(PEER_HEADS):
                thr = u1_ref[h, il:il + 1, cols]
                e1 = e1_ref[h, il:il + 1, cols]
                w = w + jnp.where(s2_ref[h, :, cols] >= thr, e2_ref[h, :, cols] * e1, 0.0)
            g_ref[rows, cols] = (_gelu(act_ref[rows, cols]) * w).astype(BF16)
    o_ref[...] += _dot(upt_ref[...], g_ref[...])


def _experts(h2t, dn, upt, s2, e2, u1, e1, tq, ni):
    d, t = h2t.shape
    ne = dn.shape[0]
    nk = s2.shape[1]
    et = ni * nk
    tok = pl.BlockSpec((PEER_HEADS, nk, tq), lambda i, e: (0, 0, i))
    sel = pl.BlockSpec((PEER_HEADS, ni, tq), lambda i, e: (0, e, i))
    return pl.pallas_call(
        functools.partial(_experts_kernel, nk=nk, ni=ni, tq=tq),
        grid=(t // tq, ne // et),
        in_specs=[pl.BlockSpec((d, tq), lambda i, e: (0, i)),
                  pl.BlockSpec((et, d), lambda i, e: (e, 0)),
                  pl.BlockSpec((d, et), lambda i, e: (0, e)),
                  tok, tok, sel, sel],
        out_specs=pl.BlockSpec((d, tq), lambda i, e: (0, i)),
        out_shape=jax.ShapeDtypeStruct((d, t), F32),
        scratch_shapes=[pltpu.VMEM((et, tq), F32), pltpu.VMEM((et, tq), BF16)],
        compiler_params=_cparams(("parallel", "arbitrary")),
        name="experts",
    )(h2t, dn, upt, s2, e2, u1, e1)


def _final_kernel(x1_ref, pt_ref, gt2_ref, g_ref, o_ref):
    x2 = x1_ref[...] + gt2_ref[0] * pt_ref[...].T
    ms = jnp.mean(x2 * x2, axis=-1, keepdims=True)
    o_ref[...] = x2 * lax.rsqrt(ms + EPS) * g_ref[...]


def _final(x1, peer_t, gt2, g, seq, tm):
    t, d = x1.shape
    per_b = seq // tm
    return pl.pallas_call(
        _final_kernel,
        grid=(t // tm,),
        in_specs=[pl.BlockSpec((tm, d), lambda i: (i, 0)),
                  pl.BlockSpec((d, tm), lambda i: (0, i)),
                  pl.BlockSpec((1, 1, d), lambda i: (i // per_b, 0, 0)),
                  pl.BlockSpec((1, d), lambda i: (0, 0))],
        out_specs=pl.BlockSpec((tm, d), lambda i: (i, 0)),
        out_shape=jax.ShapeDtypeStruct((t, d), F32),
        compiler_params=_cparams(("parallel",)),
        name="final",
    )(x1, peer_t, gt2, g)


def _t5_bucket(dist):
    dist = np.clip(dist, 0, None)
    max_exact = NUM_BUCKETS // 2
    large = max_exact + (np.log(np.maximum(dist, 1) / max_exact) / np.log(MAX_DISTANCE / max_exact)
                         * (NUM_BUCKETS - max_exact)).astype(np.int32)
    large = np.minimum(large, NUM_BUCKETS - 1)
    return np.where(dist < max_exact, dist, large).astype(np.int32)


def _row_tile(seq, want):
    tm = min(want, seq)
    assert seq % tm == 0
    return tm


def kernel(x, c, norm_mix_g, norm_ffn_g, w_ada, b_ada, w_in, b_forget, sinks, w_branch_a,
           w_branch_b, w_out, rel_bias, w_query, sub_keys, expert_down, expert_up, final_norm_g):
    bsz, seq, d = x.shape
    depth = w_ada.shape[0]
    t = bsz * seq
    n_keys = sub_keys.shape[3]
    tm = _row_tile(seq, 512)

    dist = np.arange(BLOCK)[:, None] + BLOCK - np.arange(2 * BLOCK)[None, :]
    bias = jnp.transpose(rel_bias[_t5_bucket(dist)].astype(F32), (2, 0, 1))

    o_a = SWA_WIDTH + 2 * SWA_KV_WIDTH
    o_b = o_a + 3 * FOX_WIDTH
    o_f = o_b + FOX_HEADS

    xc = x.reshape(t, d)
    for l in range(depth):
        mod = _ada(c, w_ada[l], b_ada[l])
        sh1, sc1, gt1, sh2, sc2, gt2 = [m.reshape(bsz, 1, d) for m in jnp.split(mod, N_MOD, axis=-1)]

        w = w_in[l]
        wa = w[:, :o_a].astype(BF16)
        wb = w[:, o_a:o_b].astype(BF16)
        wf = jnp.pad(w[:, o_b:o_f], ((0, 0), (0, LANES - FOX_HEADS)))
        wg = w[:, o_f:].astype(BF16)
        qa, ka, va, qb, kb, vb, fl, ga, gb = _inproj(
            xc, norm_mix_g[l].reshape(1, d), sc1, sh1, wa, wb, wf, wg, seq, tm)

        bpad = jnp.pad(b_forget[l], (0, LANES - FOX_HEADS)).reshape(1, LANES)
        fcol, frow = _cumsum(fl.reshape(bsz, seq, LANES), bpad)

        oa = _swa(qa, ka, va, sinks[l], bias, bsz, seq)
        ob = _fox(qb, kb, vb, fcol, frow, bsz, seq, _row_tile(seq, 256))

        x1, h2t, q = _mix(xc, oa, ob, ga, gb, gt1, norm_ffn_g[l].reshape(1, d), sc2, sh2,
                          w_branch_a[l].astype(BF16), w_branch_b[l].astype(BF16),
                          w_out[l].astype(BF16), w_query[l].astype(BF16), seq, tm)

        s2, e2, u1, e1 = _route(q, sub_keys[l], _row_tile(seq, 256))
        ni = 8
        peer_t = _experts(h2t, expert_down[l].astype(BF16), expert_up[l].T.astype(BF16),
                          s2, e2, u1, e1, tm, ni)

        g_next = final_norm_g.reshape(1, d)
        xc = _final(x1, peer_t, gt2, g_next, seq, tm)
        assert depth == 1
    return xc.reshape(bsz, seq, d)
```

```python
import functools
import math

import numpy as np
import jax
import jax.numpy as jnp
from jax import lax
from jax.experimental import pallas as pl
from jax.experimental.pallas import tpu as pltpu

HEAD_DIM = 64
SWA_Q_HEADS = 8
SWA_KV_HEADS = 2
SWA_GROUP = SWA_Q_HEADS // SWA_KV_HEADS
WINDOW = 128
BLOCK = 128
FOX_HEADS = 8
SWA_WIDTH = SWA_Q_HEADS * HEAD_DIM
SWA_KV_WIDTH = SWA_KV_HEADS * HEAD_DIM
FOX_WIDTH = FOX_HEADS * HEAD_DIM
NUM_BUCKETS = 32
MAX_DISTANCE = 128
PEER_HEADS = 8
PEER_TOPK = 16
PEER_HALF = 64
N_MOD = 6
EPS = 1e-6
NEG = -1e30
LANES = 128
VMEM_LIMIT = 56 * 1024 * 1024
LOG2E = math.log2(math.e)
N_FSPLIT = 3
FOX_TK = 256

F32 = jnp.float32
BF16 = jnp.bfloat16


def _cparams(sem):
    return pltpu.CompilerParams(dimension_semantics=sem, vmem_limit_bytes=VMEM_LIMIT)


def _dot(a, b):
    return jnp.dot(a, b, preferred_element_type=F32)


def _dot_nt(a, b):
    return lax.dot_general(a, b, (((1,), (1,)), ((), ())), preferred_element_type=F32)


def _rms_mod(x, g, sc, sh):
    ms = jnp.mean(x * x, axis=-1, keepdims=True)
    return (x * lax.rsqrt(ms + EPS) * g) * (1.0 + sc) + sh


def _ada_kernel(c_ref, w_ref, b_ref, o_ref):
    c = c_ref[...]
    ca = c * jax.nn.sigmoid(c)
    o_ref[...] = _dot(ca, w_ref[...]) + b_ref[...]


def _ada(c, w_ada, b_ada):
    bsz, d = c.shape
    n = w_ada.shape[1]
    tn = d
    return pl.pallas_call(
        _ada_kernel,
        grid=(n // tn,),
        in_specs=[pl.BlockSpec((bsz, d), lambda j: (0, 0)),
                  pl.BlockSpec((d, tn), lambda j: (0, j)),
                  pl.BlockSpec((1, tn), lambda j: (0, j))],
        out_specs=pl.BlockSpec((bsz, tn), lambda j: (0, j)),
        out_shape=jax.ShapeDtypeStruct((bsz, n), F32),
        compiler_params=_cparams(("arbitrary",)),
        name="ada",
    )(c, w_ada, b_ada.reshape(1, n))


def _inproj_kernel(x_ref, g_ref, sc_ref, sh_ref, wa_ref, wqt_ref, wk_ref, wvt_ref, wf_ref, wg_ref,
                   qa_ref, ka_ref, va_ref, qt_ref, kp_ref, vt_ref, fl_ref, ga_ref, gb_ref):
    h = _rms_mod(x_ref[...], g_ref[...], sc_ref[0], sh_ref[0])
    hb = h.astype(BF16)
    ht = h.T.astype(BF16)
    scale = HEAD_DIM ** -0.5
    pa = _dot(hb, wa_ref[...])
    qa_ref[...] = (pa[:, :SWA_WIDTH] * scale).astype(BF16)
    ka_ref[...] = pa[:, SWA_WIDTH:SWA_WIDTH + SWA_KV_WIDTH].astype(BF16)
    va_ref[...] = pa[:, SWA_WIDTH + SWA_KV_WIDTH:].astype(BF16)
    qt = _dot(wqt_ref[...], ht) * (scale * LOG2E)
    r = lax.broadcasted_iota(jnp.int32, qt.shape, 0) % LANES
    qt = jnp.where((r >= HEAD_DIM) & (r < HEAD_DIM + N_FSPLIT), 1.0, qt)
    qt_ref[...] = qt.astype(BF16).reshape(qt_ref.shape)
    kp_ref[...] = _dot(hb, wk_ref[...]).astype(BF16)
    vt = _dot(wvt_ref[...], ht).astype(BF16)
    tk = vt_ref.shape[2]
    for cb in range(vt_ref.shape[0]):
        vt_ref[cb] = vt[:, cb * tk:(cb + 1) * tk]
    fl_ref[...] = _dot(h, wf_ref[...])
    d = ga_ref.shape[1]
    pg = _dot(hb, wg_ref[...])
    ga_ref[...] = jax.nn.sigmoid(pg[:, :d]).astype(BF16)
    gb_ref[...] = jax.nn.sigmoid(pg[:, d:]).astype(BF16)


def _inproj(x2d, g, sc1, sh1, wa, wqt, wk, wvt, wf, wg, seq, tm):
    t, d = x2d.shape
    per_b = seq // tm
    hp = FOX_HEADS * LANES
    row = lambda w: pl.BlockSpec((tm, w), lambda i: (i, 0))
    full = lambda a: pl.BlockSpec(a.shape, lambda i: (0,) * a.ndim)
    mod = pl.BlockSpec((1, 1, d), lambda i: (i // per_b, 0, 0))
    out_shape = [jax.ShapeDtypeStruct((t, SWA_WIDTH), BF16),
                 jax.ShapeDtypeStruct((t, SWA_KV_WIDTH), BF16),
                 jax.ShapeDtypeStruct((t, SWA_KV_WIDTH), BF16),
                 jax.ShapeDtypeStruct((FOX_HEADS, LANES, t), BF16),
                 jax.ShapeDtypeStruct((t, hp), BF16),
                 jax.ShapeDtypeStruct((t // FOX_TK, FOX_WIDTH, FOX_TK), BF16),
                 jax.ShapeDtypeStruct((t, LANES), F32),
                 jax.ShapeDtypeStruct((t, d), BF16), jax.ShapeDtypeStruct((t, d), BF16)]
    out_specs = [row(SWA_WIDTH), row(SWA_KV_WIDTH), row(SWA_KV_WIDTH),
                 pl.BlockSpec((FOX_HEADS, LANES, tm), lambda i: (0, 0, i)),
                 row(hp),
                 pl.BlockSpec((tm // FOX_TK, FOX_WIDTH, FOX_TK), lambda i: (i, 0, 0)),
                 row(LANES), row(d), row(d)]
    return pl.pallas_call(
        _inproj_kernel,
        grid=(t // tm,),
        in_specs=[row(d), full(g), mod, mod, full(wa), full(wqt), full(wk), full(wvt), full(wf), full(wg)],
        out_specs=out_specs,
        out_shape=out_shape,
        compiler_params=_cparams(("parallel",)),
        name="inproj",
    )(x2d, g, sc1, sh1, wa, wqt, wk, wvt, wf, wg)


def _kf_kernel(fl_ref, b_ref, kp_ref, place_ref, ko_ref):
    z = fl_ref[0] + b_ref[...]
    lf = jnp.minimum(z, 0.0) - jnp.log1p(jnp.exp(-jnp.abs(z)))
    s = lf.shape[0]
    rows = lax.broadcasted_iota(jnp.int32, lf.shape, 0)
    acc = lf
    k = 1
    while k < s:
        shifted = pltpu.roll(acc, k, 0)
        acc = acc + jnp.where(rows >= k, shifted, 0.0)
        k *= 2
    rem = acc * (-LOG2E)
    placed = None
    for piece in range(N_FSPLIT):
        part = rem.astype(BF16)
        rem = rem - part.astype(F32)
        y = _dot(part, place_ref[piece])
        placed = y if placed is None else placed + y
    ko_ref[...] = (kp_ref[...].astype(F32) + placed).astype(BF16)


def _kf(fl3, bpad, kpad, place):
    bsz, s, _ = fl3.shape
    hp = kpad.shape[1]
    return pl.pallas_call(
        _kf_kernel,
        grid=(bsz,),
        in_specs=[pl.BlockSpec((1, s, LANES), lambda b: (b, 0, 0)),
                  pl.BlockSpec((1, LANES), lambda b: (0, 0)),
                  pl.BlockSpec((s, hp), lambda b: (b, 0)),
                  pl.BlockSpec(place.shape, lambda b: (0, 0, 0))],
        out_specs=pl.BlockSpec((s, hp), lambda b: (b, 0)),
        out_shape=jax.ShapeDtypeStruct(kpad.shape, BF16),
        compiler_params=_cparams(("parallel",)),
        name="kf",
    )(fl3, bpad, kpad, place)


def _swa_kernel(sink_ref, q_ref, kp_ref, kc_ref, vp_ref, vc_ref, bias_ref, o_ref):
    n = pl.program_id(1)
    qi = lax.broadcasted_iota(jnp.int32, (BLOCK, 2 * BLOCK), 0)
    kj = lax.broadcasted_iota(jnp.int32, (BLOCK, 2 * BLOCK), 1)
    dist = qi + BLOCK - kj
    valid = (dist >= 0) & (dist < WINDOW) & ((kj >= BLOCK) | (n > 0))
    q = q_ref[...]
    kband = jnp.concatenate([kp_ref[...], kc_ref[...]], axis=0)
    vband = jnp.concatenate([vp_ref[...], vc_ref[...]], axis=0)
    outs = []
    for h in range(SWA_Q_HEADS):
        kh = h // SWA_GROUP
        qh = q[:, h * HEAD_DIM:(h + 1) * HEAD_DIM]
        kk = kband[:, kh * HEAD_DIM:(kh + 1) * HEAD_DIM]
        vv = vband[:, kh * HEAD_DIM:(kh + 1) * HEAD_DIM]
        s = _dot_nt(qh, kk) + bias_ref[h]
        s = jnp.where(valid, s, NEG)
        sink = sink_ref[h]
        m = jnp.maximum(jnp.max(s, axis=-1, keepdims=True), sink)
        p = jnp.exp(s - m)
        denom = jnp.sum(p, axis=-1, keepdims=True) + jnp.exp(sink - m)
        o = _dot(p.astype(BF16), vv)
        outs.append(o / denom)
    o_ref[...] = jnp.concatenate(outs, axis=-1).astype(o_ref.dtype)


def _swa(qa, ka, va, sinks, bias, bsz, seq):
    nb = seq // BLOCK
    cur = lambda b, n: (b * nb + n, 0)
    prev = lambda b, n: (b * nb + jnp.maximum(n - 1, 0), 0)
    return pl.pallas_call(
        _swa_kernel,
        grid=(bsz, nb),
        in_specs=[pl.BlockSpec(memory_space=pltpu.SMEM),
                  pl.BlockSpec((BLOCK, SWA_WIDTH), cur),
                  pl.BlockSpec((BLOCK, SWA_KV_WIDTH), prev),
                  pl.BlockSpec((BLOCK, SWA_KV_WIDTH), cur),
                  pl.BlockSpec((BLOCK, SWA_KV_WIDTH), prev),
                  pl.BlockSpec((BLOCK, SWA_KV_WIDTH), cur),
                  pl.BlockSpec(bias.shape, lambda b, n: (0, 0, 0))],
        out_specs=pl.BlockSpec((BLOCK, SWA_WIDTH), cur),
        out_shape=jax.ShapeDtypeStruct(qa.shape, BF16),
        compiler_params=_cparams(("parallel", "parallel")),
        name="swa",
    )(sinks, qa, ka, ka, va, va, bias)


def _fox_kernel(q_ref, k_ref, v_ref, o_ref, m_ref, l_ref, acc_ref, *, tq, tk):
    qi = pl.program_id(1)
    m_ref[...] = jnp.full(m_ref.shape, NEG, F32)
    l_ref[...] = jnp.zeros(l_ref.shape, F32)
    acc_ref[...] = jnp.zeros(acc_ref.shape, F32)
    per_q = tq // tk

    def block(j, masked):
        krow0 = pl.multiple_of(j * tk, tk)
        if masked:
            kpos = j * tk + lax.broadcasted_iota(jnp.int32, (tk, tq), 0)
            qpos = qi * tq + lax.broadcasted_iota(jnp.int32, (tk, tq), 1)
            causal = kpos <= qpos

        def scores(h):
            kblk = k_ref[pl.ds(krow0, tk), h * LANES:(h + 1) * LANES]
            return _dot(kblk, q_ref[h])

        s_next = scores(0)
        for h in range(FOX_HEADS):
            s = s_next
            if h + 1 < FOX_HEADS:
                s_next = scores(h + 1)
            if masked:
                s = jnp.where(causal, s, NEG)
            m_old = m_ref[h:h + 1, :]
            m_new = jnp.maximum(m_old, jnp.max(s, axis=0, keepdims=True))
            alpha = jnp.exp2(m_old - m_new)
            p = jnp.exp2(s - m_new)
            l_ref[h:h + 1, :] = alpha * l_ref[h:h + 1, :] + jnp.sum(p, axis=0, keepdims=True)
            m_ref[h:h + 1, :] = m_new
            rows = slice(h * HEAD_DIM, (h + 1) * HEAD_DIM)
            pv = _dot(v_ref[j, rows, :], p.astype(BF16))
            acc_ref[rows, :] = alpha * acc_ref[rows, :] + pv

    def body(j, carry):
        block(j, False)
        return carry

    lax.fori_loop(0, qi * per_q, body, 0)
    for jj in range(per_q):
        block(qi * per_q + jj, True)

    outs = []
    for h in range(FOX_HEADS):
        rows = slice(h * HEAD_DIM, (h + 1) * HEAD_DIM)
        outs.append(acc_ref[rows, :] / l_ref[h:h + 1, :])
    o_ref[...] = jnp.concatenate(outs, axis=0).T.astype(o_ref.dtype)


def _fox(qt, kf, vt, bsz, seq, tq):
    tk = FOX_TK
    nq = seq // tq
    hp = kf.shape[1]
    return pl.pallas_call(
        functools.partial(_fox_kernel, tq=tq, tk=tk),
        grid=(bsz, nq),
        in_specs=[pl.BlockSpec((FOX_HEADS, LANES, tq), lambda b, i: (0, 0, b * nq + i)),
                  pl.BlockSpec((seq, hp), lambda b, i: (b, 0)),
                  pl.BlockSpec((seq // tk, FOX_WIDTH, tk), lambda b, i: (b, 0, 0))],
        out_specs=pl.BlockSpec((tq, FOX_WIDTH), lambda b, i: (b * nq + i, 0)),
        out_shape=jax.ShapeDtypeStruct((bsz * seq, FOX_WIDTH), BF16),
        scratch_shapes=[pltpu.VMEM((FOX_HEADS, tq), F32),
                        pltpu.VMEM((FOX_HEADS, tq), F32),
                        pltpu.VMEM((FOX_WIDTH, tq), F32)],
        compiler_params=_cparams(("parallel", "arbitrary")),
        name="fox",
    )(qt, kf, vt)


def _mix_kernel(x_ref, oa_ref, ob_ref, ga_ref, gb_ref, gt1_ref, g2_ref, sc2_ref, sh2_ref,
                wa_ref, wb_ref, wo_ref, wq_ref, x1_ref, h2t_ref, q_ref):
    ya = _dot(oa_ref[...], wa_ref[...])
    yb = _dot(ob_ref[...], wb_ref[...])
    mixed = ga_ref[...].astype(F32) * ya + gb_ref[...].astype(F32) * yb
    x1 = x_ref[...] + gt1_ref[0] * _dot(mixed.astype(BF16), wo_ref[...])
    x1_ref[...] = x1
    h2 = _rms_mod(x1, g2_ref[...], sc2_ref[0], sh2_ref[0])
    h2t_ref[...] = h2.T.astype(BF16)
    q_ref[...] = _dot(h2.astype(BF16), wq_ref[...])


def _mix(x2d, oa, ob, ga, gb, gt1, g2, sc2, sh2, wa, wb, wo, wq, seq, tm):
    t, d = x2d.shape
    per_b = seq // tm
    row = lambda w: pl.BlockSpec((tm, w), lambda i: (i, 0))
    full = lambda a: pl.BlockSpec(a.shape, lambda i: (0,) * a.ndim)
    mod = pl.BlockSpec((1, 1, d), lambda i: (i // per_b, 0, 0))
    return pl.pallas_call(
        _mix_kernel,
        grid=(t // tm,),
        in_specs=[row(d), row(SWA_WIDTH), row(FOX_WIDTH), row(d), row(d), mod, full(g2), mod, mod,
                  full(wa), full(wb), full(wo), full(wq)],
        out_specs=[row(d), pl.BlockSpec((d, tm), lambda i: (0, i)), row(wq.shape[1])],
        out_shape=[jax.ShapeDtypeStruct((t, d), F32), jax.ShapeDtypeStruct((d, t), BF16),
                   jax.ShapeDtypeStruct((t, wq.shape[1]), F32)],
        compiler_params=_cparams(("parallel",)),
        name="mix",
    )(x2d, oa, ob, ga, gb, gt1, g2, sc2, sh2, wa, wb, wo, wq)


def _top_desc(vals, count):
    tops = []
    work = vals
    for r in range(count):
        mx = jnp.max(work, axis=0, keepdims=True)
        tops.append(mx)
        if r + 1 < count:
            work = jnp.where(work == mx, -jnp.inf, work)
    return tops


def _route_kernel(q_ref, sk_ref, s2_ref, e2_ref, u1_ref, e1_ref):
    q = q_ref[...]
    for h in range(PEER_HEADS):
        q1 = q[:, (2 * h) * PEER_HALF:(2 * h + 1) * PEER_HALF]
        q2 = q[:, (2 * h + 1) * PEER_HALF:(2 * h + 2) * PEER_HALF]
        s1 = _dot_nt(sk_ref[h, 0], q1)
        s2 = _dot_nt(sk_ref[h, 1], q2)
        a = _top_desc(s1, PEER_TOPK + 1)
        b = _top_desc(s2, PEER_TOPK + 1)
        cands = [a[i] + b[j] for i in range(PEER_TOPK + 1) for j in range(PEER_TOPK + 1)
                 if (i + 1) * (j + 1) <= PEER_TOPK + 1]
        pad = (-len(cands)) % 8
        cands += [jnp.full_like(cands[0], -jnp.inf)] * pad
        c = _top_desc(jnp.concatenate(cands, axis=0), PEER_TOPK + 1)
        theta = 0.5 * (c[PEER_TOPK - 1] + c[PEER_TOPK])
        m = c[0]
        z = c[0] * 0.0
        for r in range(PEER_TOPK):
            z = z + jnp.exp(c[r] - m)
        s2_ref[h] = s2
        e2_ref[h] = jnp.exp(s2 - b[0])
        u1_ref[h] = theta - s1
        e1_ref[h] = jnp.exp(s1 - a[0]) / z


def _route(q, sub_keys, tr):
    t, w = q.shape
    nk = sub_keys.shape[2]
    spec = pl.BlockSpec((PEER_HEADS, nk, tr), lambda i: (0, 0, i))
    shp = jax.ShapeDtypeStruct((PEER_HEADS, nk, t), F32)
    return pl.pallas_call(
        _route_kernel,
        grid=(t // tr,),
        in_specs=[pl.BlockSpec((tr, w), lambda i: (i, 0)),
                  pl.BlockSpec(sub_keys.shape, lambda i: (0, 0, 0, 0))],
        out_specs=[spec] * 4,
        out_shape=[shp] * 4,
        compiler_params=_cparams(("parallel",)),
        name="route",
    )(q, sub_keys)


def _gelu(x):
    return 0.5 * x * (1.0 + lax.erf(x * (2.0 ** -0.5)))


def _experts_kernel(h2t_ref, dn_ref, upt_ref, s2_ref, e2_ref, u1_ref, e1_ref, o_ref,
                    act_ref, g_ref, *, nk, ni, tq):
    e = pl.program_id(1)

    @pl.when(e == 0)
    def _():
        o_ref[...] = jnp.zeros(o_ref.shape, F32)

    act_ref[...] = _dot(dn_ref[...], h2t_ref[...])
    for il in range(ni):
        rows = slice(il * nk, (il + 1) * nk)
        for c in range(tq // LANES):
            cols = slice(c * LANES, (c + 1) * LANES)
            w = jnp.zeros((nk, LANES), F32)
            for h in range(PEER_HEADS):
                thr = u1_ref[h, il:il + 1, cols]
                e1 = e1_ref[h, il:il + 1, cols]
                w = w + jnp.where(s2_ref[h, :, cols] >= thr, e2_ref[h, :, cols] * e1, 0.0)
            g_ref[rows, cols] = (_gelu(act_ref[rows, cols]) * w).astype(BF16)
    o_ref[...] += _dot(upt_ref[...], g_ref[...])


def _experts(h2t, dn, upt, s2, e2, u1, e1, tq, ni):
    d, t = h2t.shape
    ne = dn.shape[0]
    nk = s2.shape[1]
    et = ni * nk
    tok = pl.BlockSpec((PEER_HEADS, nk, tq), lambda i, e: (0, 0, i))
    sel = pl.BlockSpec((PEER_HEADS, ni, tq), lambda i, e: (0, e, i))
    return pl.pallas_call(
        functools.partial(_experts_kernel, nk=nk, ni=ni, tq=tq),
        grid=(t // tq, ne // et),
        in_specs=[pl.BlockSpec((d, tq), lambda i, e: (0, i)),
                  pl.BlockSpec((et, d), lambda i, e: (e, 0)),
                  pl.BlockSpec((d, et), lambda i, e: (0, e)),
                  tok, tok, sel, sel],
        out_specs=pl.BlockSpec((d, tq), lambda i, e: (0, i)),
        out_shape=jax.ShapeDtypeStruct((d, t), F32),
        scratch_shapes=[pltpu.VMEM((et, tq), F32), pltpu.VMEM((et, tq), BF16)],
        compiler_params=_cparams(("parallel", "arbitrary")),
        name="experts",
    )(h2t, dn, upt, s2, e2, u1, e1)


def _final_kernel(x1_ref, pt_ref, gt2_ref, g_ref, o_ref):
    x2 = x1_ref[...] + gt2_ref[0] * pt_ref[...].T
    ms = jnp.mean(x2 * x2, axis=-1, keepdims=True)
    o_ref[...] = x2 * lax.rsqrt(ms + EPS) * g_ref[...]


def _final(x1, peer_t, gt2, g, seq, tm):
    t, d = x1.shape
    per_b = seq // tm
    return pl.pallas_call(
        _final_kernel,
        grid=(t // tm,),
        in_specs=[pl.BlockSpec((tm, d), lambda i: (i, 0)),
                  pl.BlockSpec((d, tm), lambda i: (0, i)),
                  pl.BlockSpec((1, 1, d), lambda i: (i // per_b, 0, 0)),
                  pl.BlockSpec((1, d), lambda i: (0, 0))],
        out_specs=pl.BlockSpec((tm, d), lambda i: (i, 0)),
        out_shape=jax.ShapeDtypeStruct((t, d), F32),
        compiler_params=_cparams(("parallel",)),
        name="final",
    )(x1, peer_t, gt2, g)


def _t5_bucket(dist):
    dist = np.clip(dist, 0, None)
    max_exact = NUM_BUCKETS // 2
    large = max_exact + (np.log(np.maximum(dist, 1) / max_exact) / np.log(MAX_DISTANCE / max_exact)
                         * (NUM_BUCKETS - max_exact)).astype(np.int32)
    large = np.minimum(large, NUM_BUCKETS - 1)
    return np.where(dist < max_exact, dist, large).astype(np.int32)


def _row_tile(seq, want):
    tm = min(want, seq)
    assert seq % tm == 0
    return tm


def _fsplit_placement():
    place = np.zeros((N_FSPLIT, LANES, FOX_HEADS * LANES), np.float32)
    for p in range(N_FSPLIT):
        for h in range(FOX_HEADS):
            place[p, h, h * LANES + HEAD_DIM + p] = 1.0
    return jnp.asarray(place, BF16)


def kernel(x, c, norm_mix_g, norm_ffn_g, w_ada, b_ada, w_in, b_forget, sinks, w_branch_a,
           w_branch_b, w_out, rel_bias, w_query, sub_keys, expert_down, expert_up, final_norm_g):
    bsz, seq, d = x.shape
    depth = w_ada.shape[0]
    t = bsz * seq
    tm = _row_tile(seq, 512)
    assert seq % FOX_TK == 0 and tm % FOX_TK == 0

    dist = np.arange(BLOCK)[:, None] + BLOCK - np.arange(2 * BLOCK)[None, :]
    bias = jnp.transpose(rel_bias[_t5_bucket(dist)].astype(F32), (2, 0, 1))
    place = _fsplit_placement()

    o_a = SWA_WIDTH + 2 * SWA_KV_WIDTH
    o_q, o_k, o_v = o_a, o_a + FOX_WIDTH, o_a + 2 * FOX_WIDTH
    o_b = o_a + 3 * FOX_WIDTH
    o_f = o_b + FOX_HEADS
    pad_head = LANES - HEAD_DIM

    xc = x.reshape(t, d)
    for l in range(depth):
        mod = _ada(c, w_ada[l], b_ada[l])
        sh1, sc1, gt1, sh2, sc2, gt2 = [m.reshape(bsz, 1, d) for m in jnp.split(mod, N_MOD, axis=-1)]

        w = w_in[l]
        wa = w[:, :o_a].astype(BF16)
        wq3 = w[:, o_q:o_k].T.reshape(FOX_HEADS, HEAD_DIM, d)
        wqt = jnp.pad(wq3, ((0, 0), (0, pad_head), (0, 0))).reshape(FOX_HEADS * LANES, d).astype(BF16)
        wk3 = w[:, o_k:o_v].reshape(d, FOX_HEADS, HEAD_DIM)
        wk = jnp.pad(wk3, ((0, 0), (0, 0), (0, pad_head))).reshape(d, FOX_HEADS * LANES).astype(BF16)
        wvt = w[:, o_v:o_b].T.astype(BF16)
        wf = jnp.pad(w[:, o_b:o_f], ((0, 0), (0, LANES - FOX_HEADS)))
        wg = w[:, o_f:].astype(BF16)
        qa, ka, va, qt, kpad, vt, fl, ga, gb = _inproj(
            xc, norm_mix_g[l].reshape(1, d), sc1, sh1, wa, wqt, wk, wvt, wf, wg, seq, tm)

        bpad = jnp.pad(b_forget[l], (0, LANES - FOX_HEADS)).reshape(1, LANES)
        kf = _kf(fl.reshape(bsz, seq, LANES), bpad, kpad, place)

        oa = _swa(qa, ka, va, sinks[l], bias, bsz, seq)
        ob = _fox(qt, kf, vt, bsz, seq, _row_tile(seq, 256))

        x1, h2t, q = _mix(xc, oa, ob, ga, gb, gt1, norm_ffn_g[l].reshape(1, d), sc2, sh2,
                          w_branch_a[l].astype(BF16), w_branch_b[l].astype(BF16),
                          w_out[l].astype(BF16), w_query[l].astype(BF16), seq, tm)

        s2, e2, u1, e1 = _route(q, sub_keys[l], _row_tile(seq, 256))
        ni = 8
        peer_t = _experts(h2t, expert_down[l].astype(BF16), expert_up[l].T.astype(BF16),
                          s2, e2, u1, e1, tm, ni)

        g_next = final_norm_g.reshape(1, d)
        xc = _final(x1, peer_t, gt2, g_next, seq, tm)
        assert depth == 1
    return xc.reshape(bsz, seq, d)
```

```python
import functools
import math

import numpy as np
import jax
import jax.numpy as jnp
from jax import lax
from jax.experimental import pallas as pl
from jax.experimental.pallas import tpu as pltpu

HEAD_DIM = 64
SWA_Q_HEADS = 8
SWA_KV_HEADS = 2
SWA_GROUP = SWA_Q_HEADS // SWA_KV_HEADS
WINDOW = 128
BLOCK = 128
FOX_HEADS = 8
SWA_WIDTH = SWA_Q_HEADS * HEAD_DIM
SWA_KV_WIDTH = SWA_KV_HEADS * HEAD_DIM
FOX_WIDTH = FOX_HEADS * HEAD_DIM
NUM_BUCKETS = 32
MAX_DISTANCE = 128
PEER_HEADS = 8
PEER_TOPK = 16
PEER_HALF = 64
N_MOD = 6
EPS = 1e-6
NEG = -1e30
LANES = 128
VMEM_LIMIT = 56 * 1024 * 1024
LOG2E = math.log2(math.e)
N_FSPLIT = 3
FOX_TK = 256
MXU_PIECE_ROWS = 512
MXU_PIECE_COLS = 256

F32 = jnp.float32
BF16 = jnp.bfloat16


def _cparams(sem):
    return pltpu.CompilerParams(dimension_semantics=sem, vmem_limit_bytes=VMEM_LIMIT)


def _dot(a, b):
    return jnp.dot(a, b, preferred_element_type=F32)


def _dot_nt(a, b):
    return lax.dot_general(a, b, (((1,), (1,)), ((), ())), preferred_element_type=F32)


def _rms_mod(x, g, sc, sh):
    ms = jnp.mean(x * x, axis=-1, keepdims=True)
    return (x * lax.rsqrt(ms + EPS) * g) * (1.0 + sc) + sh


def _ada_kernel(c_ref, w_ref, b_ref, o_ref):
    c = c_ref[...]
    ca = c * jax.nn.sigmoid(c)
    o_ref[...] = _dot(ca, w_ref[...]) + b_ref[...]


def _ada(c, w_ada, b_ada):
    bsz, d = c.shape
    n = w_ada.shape[1]
    tn = d
    return pl.pallas_call(
        _ada_kernel,
        grid=(n // tn,),
        in_specs=[pl.BlockSpec((bsz, d), lambda j: (0, 0)),
                  pl.BlockSpec((d, tn), lambda j: (0, j)),
                  pl.BlockSpec((1, tn), lambda j: (0, j))],
        out_specs=pl.BlockSpec((bsz, tn), lambda j: (0, j)),
        out_shape=jax.ShapeDtypeStruct((bsz, n), F32),
        compiler_params=_cparams(("arbitrary",)),
        name="ada",
    )(c, w_ada, b_ada.reshape(1, n))


def _inproj_kernel(x_ref, g_ref, sc_ref, sh_ref, wa_ref, wqt_ref, wk_ref, wvt_ref, wf_ref, wg_ref,
                   qa_ref, ka_ref, va_ref, qt_ref, kp_ref, vt_ref, fl_ref, ga_ref, gb_ref):
    h = _rms_mod(x_ref[...], g_ref[...], sc_ref[0], sh_ref[0])
    hb = h.astype(BF16)
    ht = h.T.astype(BF16)
    scale = HEAD_DIM ** -0.5
    pa = _dot(hb, wa_ref[...])
    qa_ref[...] = (pa[:, :SWA_WIDTH] * scale).astype(BF16)
    ka_ref[...] = pa[:, SWA_WIDTH:SWA_WIDTH + SWA_KV_WIDTH].astype(BF16)
    va_ref[...] = pa[:, SWA_WIDTH + SWA_KV_WIDTH:].astype(BF16)
    qt = _dot(wqt_ref[...], ht) * (scale * LOG2E)
    r = lax.broadcasted_iota(jnp.int32, qt.shape, 0) % LANES
    qt = jnp.where((r >= HEAD_DIM) & (r < HEAD_DIM + N_FSPLIT), 1.0, qt)
    qt_ref[...] = qt.astype(BF16).reshape(qt_ref.shape)
    kp_ref[...] = _dot(hb, wk_ref[...]).astype(BF16)
    vt = _dot(wvt_ref[...], ht).astype(BF16)
    tk = vt_ref.shape[2]
    for cb in range(vt_ref.shape[0]):
        vt_ref[cb] = vt[:, cb * tk:(cb + 1) * tk]
    fl_ref[...] = _dot(h, wf_ref[...])
    d = ga_ref.shape[1]
    pg = _dot(hb, wg_ref[...])
    ga_ref[...] = jax.nn.sigmoid(pg[:, :d]).astype(BF16)
    gb_ref[...] = jax.nn.sigmoid(pg[:, d:]).astype(BF16)


def _inproj(x2d, g, sc1, sh1, wa, wqt, wk, wvt, wf, wg, seq, tm):
    t, d = x2d.shape
    per_b = seq // tm
    hp = FOX_HEADS * LANES
    row = lambda w: pl.BlockSpec((tm, w), lambda i: (i, 0))
    full = lambda a: pl.BlockSpec(a.shape, lambda i: (0,) * a.ndim)
    mod = pl.BlockSpec((1, 1, d), lambda i: (i // per_b, 0, 0))
    out_shape = [jax.ShapeDtypeStruct((t, SWA_WIDTH), BF16),
                 jax.ShapeDtypeStruct((t, SWA_KV_WIDTH), BF16),
                 jax.ShapeDtypeStruct((t, SWA_KV_WIDTH), BF16),
                 jax.ShapeDtypeStruct((FOX_HEADS, LANES, t), BF16),
                 jax.ShapeDtypeStruct((t, hp), BF16),
                 jax.ShapeDtypeStruct((t // FOX_TK, FOX_WIDTH, FOX_TK), BF16),
                 jax.ShapeDtypeStruct((t, LANES), F32),
                 jax.ShapeDtypeStruct((t, d), BF16), jax.ShapeDtypeStruct((t, d), BF16)]
    out_specs = [row(SWA_WIDTH), row(SWA_KV_WIDTH), row(SWA_KV_WIDTH),
                 pl.BlockSpec((FOX_HEADS, LANES, tm), lambda i: (0, 0, i)),
                 row(hp),
                 pl.BlockSpec((tm // FOX_TK, FOX_WIDTH, FOX_TK), lambda i: (i, 0, 0)),
                 row(LANES), row(d), row(d)]
    return pl.pallas_call(
        _inproj_kernel,
        grid=(t // tm,),
        in_specs=[row(d), full(g), mod, mod, full(wa), full(wqt), full(wk), full(wvt), full(wf), full(wg)],
        out_specs=out_specs,
        out_shape=out_shape,
        compiler_params=_cparams(("parallel",)),
        name="inproj",
    )(x2d, g, sc1, sh1, wa, wqt, wk, wvt, wf, wg)


def _kf_kernel(fl_ref, b_ref, kp_ref, place_ref, ko_ref):
    z = fl_ref[0] + b_ref[...]
    lf = jnp.minimum(z, 0.0) - jnp.log1p(jnp.exp(-jnp.abs(z)))
    s = lf.shape[0]
    rows = lax.broadcasted_iota(jnp.int32, lf.shape, 0)
    acc = lf
    k = 1
    while k < s:
        shifted = pltpu.roll(acc, k, 0)
        acc = acc + jnp.where(rows >= k, shifted, 0.0)
        k *= 2
    rem = acc * (-LOG2E)
    placed = None
    for piece in range(N_FSPLIT):
        part = rem.astype(BF16)
        rem = rem - part.astype(F32)
        y = _dot(part, place_ref[piece])
        placed = y if placed is None else placed + y
    ko_ref[...] = (kp_ref[...].astype(F32) + placed).astype(BF16)


def _kf(fl3, bpad, kpad, place):
    bsz, s, _ = fl3.shape
    hp = kpad.shape[1]
    return pl.pallas_call(
        _kf_kernel,
        grid=(bsz,),
        in_specs=[pl.BlockSpec((1, s, LANES), lambda b: (b, 0, 0)),
                  pl.BlockSpec((1, LANES), lambda b: (0, 0)),
                  pl.BlockSpec((s, hp), lambda b: (b, 0)),
                  pl.BlockSpec(place.shape, lambda b: (0, 0, 0))],
        out_specs=pl.BlockSpec((s, hp), lambda b: (b, 0)),
        out_shape=jax.ShapeDtypeStruct(kpad.shape, BF16),
        compiler_params=_cparams(("parallel",)),
        name="kf",
    )(fl3, bpad, kpad, place)


def _swa_kernel(sink_ref, q_ref, kp_ref, kc_ref, vp_ref, vc_ref, bias_ref, o_ref):
    n = pl.program_id(1)
    qi = lax.broadcasted_iota(jnp.int32, (BLOCK, 2 * BLOCK), 0)
    kj = lax.broadcasted_iota(jnp.int32, (BLOCK, 2 * BLOCK), 1)
    dist = qi + BLOCK - kj
    valid = (dist >= 0) & (dist < WINDOW) & ((kj >= BLOCK) | (n > 0))
    q = q_ref[...]
    kband = jnp.concatenate([kp_ref[...], kc_ref[...]], axis=0)
    vband = jnp.concatenate([vp_ref[...], vc_ref[...]], axis=0)
    outs = []
    for h in range(SWA_Q_HEADS):
        kh = h // SWA_GROUP
        qh = q[:, h * HEAD_DIM:(h + 1) * HEAD_DIM]
        kk = kband[:, kh * HEAD_DIM:(kh + 1) * HEAD_DIM]
        vv = vband[:, kh * HEAD_DIM:(kh + 1) * HEAD_DIM]
        s = _dot_nt(qh, kk) + bias_ref[h]
        s = jnp.where(valid, s, NEG)
        sink = sink_ref[h]
        m = jnp.maximum(jnp.max(s, axis=-1, keepdims=True), sink)
        p = jnp.exp(s - m)
        denom = jnp.sum(p, axis=-1, keepdims=True) + jnp.exp(sink - m)
        o = _dot(p.astype(BF16), vv)
        outs.append(o / denom)
    o_ref[...] = jnp.concatenate(outs, axis=-1).astype(o_ref.dtype)


def _swa(qa, ka, va, sinks, bias, bsz, seq):
    nb = seq // BLOCK
    cur = lambda b, n: (b * nb + n, 0)
    prev = lambda b, n: (b * nb + jnp.maximum(n - 1, 0), 0)
    return pl.pallas_call(
        _swa_kernel,
        grid=(bsz, nb),
        in_specs=[pl.BlockSpec(memory_space=pltpu.SMEM),
                  pl.BlockSpec((BLOCK, SWA_WIDTH), cur),
                  pl.BlockSpec((BLOCK, SWA_KV_WIDTH), prev),
                  pl.BlockSpec((BLOCK, SWA_KV_WIDTH), cur),
                  pl.BlockSpec((BLOCK, SWA_KV_WIDTH), prev),
                  pl.BlockSpec((BLOCK, SWA_KV_WIDTH), cur),
                  pl.BlockSpec(bias.shape, lambda b, n: (0, 0, 0))],
        out_specs=pl.BlockSpec((BLOCK, SWA_WIDTH), cur),
        out_shape=jax.ShapeDtypeStruct(qa.shape, BF16),
        compiler_params=_cparams(("parallel", "parallel")),
        name="swa",
    )(sinks, qa, ka, ka, va, va, bias)


def _fox_kernel(q_ref, k_ref, v_ref, o_ref, m_ref, l_ref, acc_ref, *, tq, tk):
    qi = pl.program_id(1)
    m_ref[...] = jnp.full(m_ref.shape, NEG, F32)
    l_ref[...] = jnp.zeros(l_ref.shape, F32)
    acc_ref[...] = jnp.zeros(acc_ref.shape, F32)
    per_q = tq // tk

    def block(j, masked):
        krow0 = pl.multiple_of(j * tk, tk)
        if masked:
            kpos = j * tk + lax.broadcasted_iota(jnp.int32, (tk, tq), 0)
            qpos = qi * tq + lax.broadcasted_iota(jnp.int32, (tk, tq), 1)
            causal = kpos <= qpos

        def scores(h):
            kblk = k_ref[pl.ds(krow0, tk), h * LANES:(h + 1) * LANES]
            return _dot(kblk, q_ref[h])

        s_next = scores(0)
        for h in range(FOX_HEADS):
            s = s_next
            if h + 1 < FOX_HEADS:
                s_next = scores(h + 1)
            if masked:
                s = jnp.where(causal, s, NEG)
            m_old = m_ref[h:h + 1, :]
            m_new = jnp.maximum(m_old, jnp.max(s, axis=0, keepdims=True))
            alpha = jnp.exp2(m_old - m_new)
            p = jnp.exp2(s - m_new)
            l_ref[h:h + 1, :] = alpha * l_ref[h:h + 1, :] + jnp.sum(p, axis=0, keepdims=True)
            m_ref[h:h + 1, :] = m_new
            rows = slice(h * HEAD_DIM, (h + 1) * HEAD_DIM)
            pv = _dot(v_ref[j, rows, :], p.astype(BF16))
            acc_ref[rows, :] = alpha * acc_ref[rows, :] + pv

    def body(j, carry):
        block(j, False)
        return carry

    lax.fori_loop(0, qi * per_q, body, 0)
    for jj in range(per_q):
        block(qi * per_q + jj, True)

    outs = []
    for h in range(FOX_HEADS):
        rows = slice(h * HEAD_DIM, (h + 1) * HEAD_DIM)
        outs.append(acc_ref[rows, :] / l_ref[h:h + 1, :])
    o_ref[...] = jnp.concatenate(outs, axis=0).T.astype(o_ref.dtype)


def _fox(qt, kf, vt, bsz, seq, tq):
    tk = FOX_TK
    nq = seq // tq
    hp = kf.shape[1]
    return pl.pallas_call(
        functools.partial(_fox_kernel, tq=tq, tk=tk),
        grid=(bsz, nq),
        in_specs=[pl.BlockSpec((FOX_HEADS, LANES, tq), lambda b, i: (0, 0, b * nq + i)),
                  pl.BlockSpec((seq, hp), lambda b, i: (b, 0)),
                  pl.BlockSpec((seq // tk, FOX_WIDTH, tk), lambda b, i: (b, 0, 0))],
        out_specs=pl.BlockSpec((tq, FOX_WIDTH), lambda b, i: (b * nq + i, 0)),
        out_shape=jax.ShapeDtypeStruct((bsz * seq, FOX_WIDTH), BF16),
        scratch_shapes=[pltpu.VMEM((FOX_HEADS, tq), F32),
                        pltpu.VMEM((FOX_HEADS, tq), F32),
                        pltpu.VMEM((FOX_WIDTH, tq), F32)],
        compiler_params=_cparams(("parallel", "arbitrary")),
        name="fox",
    )(qt, kf, vt)


def _mix_kernel(x_ref, oa_ref, ob_ref, ga_ref, gb_ref, gt1_ref, g2_ref, sc2_ref, sh2_ref,
                wa_ref, wb_ref, wo_ref, wq_ref, x1_ref, h2t_ref, q_ref):
    ya = _dot(oa_ref[...], wa_ref[...])
    yb = _dot(ob_ref[...], wb_ref[...])
    mixed = ga_ref[...].astype(F32) * ya + gb_ref[...].astype(F32) * yb
    x1 = x_ref[...] + gt1_ref[0] * _dot(mixed.astype(BF16), wo_ref[...])
    x1_ref[...] = x1
    h2 = _rms_mod(x1, g2_ref[...], sc2_ref[0], sh2_ref[0])
    h2t_ref[...] = h2.T.astype(BF16)
    q_ref[...] = _dot(h2.astype(BF16), wq_ref[...])


def _mix(x2d, oa, ob, ga, gb, gt1, g2, sc2, sh2, wa, wb, wo, wq, seq, tm):
    t, d = x2d.shape
    per_b = seq // tm
    row = lambda w: pl.BlockSpec((tm, w), lambda i: (i, 0))
    full = lambda a: pl.BlockSpec(a.shape, lambda i: (0,) * a.ndim)
    mod = pl.BlockSpec((1, 1, d), lambda i: (i // per_b, 0, 0))
    return pl.pallas_call(
        _mix_kernel,
        grid=(t // tm,),
        in_specs=[row(d), row(SWA_WIDTH), row(FOX_WIDTH), row(d), row(d), mod, full(g2), mod, mod,
                  full(wa), full(wb), full(wo), full(wq)],
        out_specs=[row(d), pl.BlockSpec((d, tm), lambda i: (0, i)), row(wq.shape[1])],
        out_shape=[jax.ShapeDtypeStruct((t, d), F32), jax.ShapeDtypeStruct((d, t), BF16),
                   jax.ShapeDtypeStruct((t, wq.shape[1]), F32)],
        compiler_params=_cparams(("parallel",)),
        name="mix",
    )(x2d, oa, ob, ga, gb, gt1, g2, sc2, sh2, wa, wb, wo, wq)


def _top_desc(vals, count):
    tops = []
    work = vals
    for r in range(count):
        mx = jnp.max(work, axis=0, keepdims=True)
        tops.append(mx)
        if r + 1 < count:
            work = jnp.where(work == mx, -jnp.inf, work)
    return tops


def _sort_network(n):
    def merge(lo, hi, r):
        step = r * 2
        if step < hi - lo:
            yield from merge(lo, hi, step)
            yield from merge(lo + r, hi, step)
            yield from [(i, i + r) for i in range(lo + r, hi - r, step)]
        else:
            yield (lo, lo + r)

    def sort(lo, hi):
        if hi - lo >= 1:
            mid = lo + (hi - lo) // 2
            yield from sort(lo, mid)
            yield from sort(mid + 1, hi)
            yield from merge(lo, hi, 1)

    return list(sort(0, n - 1))


def _top_desc_sorted(vals, count):
    n = vals.shape[0]
    depth = n // 8
    lists = [vals[8 * r:8 * (r + 1), :] for r in range(depth)]
    for a, b in _sort_network(depth):
        hi, lo = jnp.maximum(lists[a], lists[b]), jnp.minimum(lists[a], lists[b])
        lists[a], lists[b] = hi, lo
    tops = []
    for k in range(count):
        mx = jnp.max(lists[0], axis=0, keepdims=True)
        tops.append(mx)
        need = min(depth, count - 1 - k)
        if need == 0:
            break
        pop = lists[0] == mx
        for r in range(need):
            nxt = lists[r + 1] if r + 1 < depth else -jnp.inf
            lists[r] = jnp.where(pop, nxt, lists[r])
    return tops


def _route_kernel(q_ref, sk_ref, e2_ref, e1_ref, tau_ref):
    q = q_ref[...]
    nk = sk_ref.shape[2]
    top = _top_desc_sorted if nk % 8 == 0 and (nk // 8) & (nk // 8 - 1) == 0 else _top_desc
    for h in range(PEER_HEADS):
        q1 = q[:, (2 * h) * PEER_HALF:(2 * h + 1) * PEER_HALF]
        q2 = q[:, (2 * h + 1) * PEER_HALF:(2 * h + 2) * PEER_HALF]
        s1 = _dot_nt(sk_ref[h, 0], q1)
        s2 = _dot_nt(sk_ref[h, 1], q2)
        a = top(s1, PEER_TOPK + 1)
        b = top(s2, PEER_TOPK + 1)
        cands = [a[i] + b[j] for i in range(PEER_TOPK + 1) for j in range(PEER_TOPK + 1)
                 if (i + 1) * (j + 1) <= PEER_TOPK + 1]
        pad = (-len(cands)) % 8
        cands += [jnp.full_like(cands[0], -jnp.inf)] * pad
        c = _top_desc(jnp.concatenate(cands, axis=0), PEER_TOPK + 1)
        theta = 0.5 * (c[PEER_TOPK - 1] + c[PEER_TOPK])
        m = c[0]
        z = c[0] * 0.0
        for r in range(PEER_TOPK):
            z = z + jnp.exp(c[r] - m)
        half_inv_z = 0.5 / z
        e2_ref[h] = jnp.exp(s2 - b[0])
        e1_ref[h] = jnp.exp(s1 - a[0]) * half_inv_z
        tau_ref[h:h + 1, :] = jnp.exp(theta - m) * half_inv_z


def _route(q, sub_keys, tr):
    t, w = q.shape
    nk = sub_keys.shape[2]
    spec = pl.BlockSpec((PEER_HEADS, nk, tr), lambda i: (0, 0, i))
    shp = jax.ShapeDtypeStruct((PEER_HEADS, nk, t), F32)
    return pl.pallas_call(
        _route_kernel,
        grid=(t // tr,),
        in_specs=[pl.BlockSpec((tr, w), lambda i: (i, 0)),
                  pl.BlockSpec(sub_keys.shape, lambda i: (0, 0, 0, 0))],
        out_specs=[spec, spec, pl.BlockSpec((PEER_HEADS, tr), lambda i: (0, i))],
        out_shape=[shp, shp, jax.ShapeDtypeStruct((PEER_HEADS, t), F32)],
        compiler_params=_cparams(("parallel",)),
        name="route",
    )(q, sub_keys)


def _experts_kernel(h2t_ref, dn_ref, upt_ref, e2_ref, e1_ref, tau_ref, o_ref, act_ref, g_ref,
                    *, nk, ni, tq, ne, n_items):
    s = pl.program_id(0)
    c_item = jnp.clip(s - 2, 0, n_items - 1)

    @pl.when(s == 0)
    def _():
        act_ref[...] = jnp.zeros(act_ref.shape, F32)
        g_ref[...] = jnp.zeros(g_ref.shape, BF16)

    @pl.when(c_item % ne == 0)
    def _():
        o_ref[...] = jnp.zeros(o_ref.shape, F32)

    slot_a = s % 2
    slot_b = (s + 1) % 2
    et = ni * nk
    d = o_ref.shape[0]
    inv_sqrt2 = 2.0 ** -0.5
    mrow = min(MXU_PIECE_ROWS, et, d)
    ncol = min(MXU_PIECE_COLS, tq)

    def piece_a(mr, nc):
        rows, cols = slice(mr * mrow, (mr + 1) * mrow), slice(nc * ncol, (nc + 1) * ncol)
        act_ref[slot_a, rows, cols] = _dot(dn_ref[rows, :], h2t_ref[:, cols])

    def piece_c(mr, nc):
        rows, cols = slice(mr * mrow, (mr + 1) * mrow), slice(nc * ncol, (nc + 1) * ncol)
        o_ref[rows, cols] += _dot(upt_ref[rows, :], g_ref[slot_a, :, cols])

    def block_b(il, c):
        rows, cols = slice(il * nk, (il + 1) * nk), slice(c * LANES, (c + 1) * LANES)
        w = None
        for h in range(PEER_HEADS):
            p = e2_ref[h, :, cols] * e1_ref[h, il:il + 1, cols]
            sel = jnp.where(p >= tau_ref[h:h + 1, cols], p, 0.0)
            w = sel if w is None else w + sel
        x = act_ref[slot_b, rows, cols]
        g_ref[slot_b, rows, cols] = (x * (1.0 + lax.erf(x * inv_sqrt2)) * w).astype(BF16)

    pieces = []
    for nc in range(tq // ncol):
        for mr in range(max(et, d) // mrow):
            if mr < et // mrow:
                pieces.append(functools.partial(piece_a, mr, nc))
            if mr < d // mrow:
                pieces.append(functools.partial(piece_c, mr, nc))
    blocks = [(il, c) for il in range(ni) for c in range(tq // LANES)]
    per_piece = -(-len(blocks) // len(pieces))
    for k, (il, c) in enumerate(blocks):
        if k % per_piece == 0 and k // per_piece < len(pieces):
            pieces[k // per_piece]()
        block_b(il, c)
    for k in range(-(-len(blocks) // per_piece), len(pieces)):
        pieces[k]()


def _experts(h2t, dn, upt, e2, e1, tau, tq, ni):
    d, t = h2t.shape
    n_exp = dn.shape[0]
    nk = e2.shape[1]
    et = ni * nk
    ne = n_exp // et
    n_items = (t // tq) * ne
    a_item = lambda s: jnp.minimum(s, n_items - 1)
    b_item = lambda s: jnp.clip(s - 1, 0, n_items - 1)
    c_item = lambda s: jnp.clip(s - 2, 0, n_items - 1)
    return pl.pallas_call(
        functools.partial(_experts_kernel, nk=nk, ni=ni, tq=tq, ne=ne, n_items=n_items),
        grid=(n_items + 2,),
        in_specs=[pl.BlockSpec((d, tq), lambda s: (0, a_item(s) // ne)),
                  pl.BlockSpec((et, d), lambda s: (a_item(s) % ne, 0)),
                  pl.BlockSpec((d, et), lambda s: (0, c_item(s) % ne)),
                  pl.BlockSpec((PEER_HEADS, nk, tq), lambda s: (0, 0, b_item(s) // ne)),
                  pl.BlockSpec((PEER_HEADS, ni, tq), lambda s: (0, b_item(s) % ne, b_item(s) // ne)),
                  pl.BlockSpec((PEER_HEADS, tq), lambda s: (0, b_item(s) // ne))],
        out_specs=pl.BlockSpec((d, tq), lambda s: (0, c_item(s) // ne)),
        out_shape=jax.ShapeDtypeStruct((d, t), F32),
        scratch_shapes=[pltpu.VMEM((2, et, tq), F32), pltpu.VMEM((2, et, tq), BF16)],
        compiler_params=_cparams(("arbitrary",)),
        name="experts",
    )(h2t, dn, upt, e2, e1, tau)


def _final_kernel(x1_ref, pt_ref, gt2_ref, g_ref, o_ref):
    x2 = x1_ref[...] + gt2_ref[0] * pt_ref[...].T
    ms = jnp.mean(x2 * x2, axis=-1, keepdims=True)
    o_ref[...] = x2 * lax.rsqrt(ms + EPS) * g_ref[...]


def _final(x1, peer_t, gt2, g, seq, tm):
    t, d = x1.shape
    per_b = seq // tm
    return pl.pallas_call(
        _final_kernel,
        grid=(t // tm,),
        in_specs=[pl.BlockSpec((tm, d), lambda i: (i, 0)),
                  pl.BlockSpec((d, tm), lambda i: (0, i)),
                  pl.BlockSpec((1, 1, d), lambda i: (i // per_b, 0, 0)),
                  pl.BlockSpec((1, d), lambda i: (0, 0))],
        out_specs=pl.BlockSpec((tm, d), lambda i: (i, 0)),
        out_shape=jax.ShapeDtypeStruct((t, d), F32),
        compiler_params=_cparams(("parallel",)),
        name="final",
    )(x1, peer_t, gt2, g)


def _t5_bucket(dist):
    dist = np.clip(dist, 0, None)
    max_exact = NUM_BUCKETS // 2
    large = max_exact + (np.log(np.maximum(dist, 1) / max_exact) / np.log(MAX_DISTANCE / max_exact)
                         * (NUM_BUCKETS - max_exact)).astype(np.int32)
    large = np.minimum(large, NUM_BUCKETS - 1)
    return np.where(dist < max_exact, dist, large).astype(np.int32)


def _row_tile(seq, want):
    tm = min(want, seq)
    assert seq % tm == 0
    return tm


def _fsplit_placement():
    place = np.zeros((N_FSPLIT, LANES, FOX_HEADS * LANES), np.float32)
    for p in range(N_FSPLIT):
        for h in range(FOX_HEADS):
            place[p, h, h * LANES + HEAD_DIM + p] = 1.0
    return jnp.asarray(place, BF16)


def kernel(x, c, norm_mix_g, norm_ffn_g, w_ada, b_ada, w_in, b_forget, sinks, w_branch_a,
           w_branch_b, w_out, rel_bias, w_query, sub_keys, expert_down, expert_up, final_norm_g):
    bsz, seq, d = x.shape
    depth = w_ada.shape[0]
    t = bsz * seq
    tm = _row_tile(seq, 512)
    assert seq % FOX_TK == 0 and tm % FOX_TK == 0

    dist = np.arange(BLOCK)[:, None] + BLOCK - np.arange(2 * BLOCK)[None, :]
    bias = jnp.transpose(rel_bias[_t5_bucket(dist)].astype(F32), (2, 0, 1))
    place = _fsplit_placement()

    o_a = SWA_WIDTH + 2 * SWA_KV_WIDTH
    o_q, o_k, o_v = o_a, o_a + FOX_WIDTH, o_a + 2 * FOX_WIDTH
    o_b = o_a + 3 * FOX_WIDTH
    o_f = o_b + FOX_HEADS
    pad_head = LANES - HEAD_DIM

    xc = x.reshape(t, d)
    for l in range(depth):
        mod = _ada(c, w_ada[l], b_ada[l])
        sh1, sc1, gt1, sh2, sc2, gt2 = [m.reshape(bsz, 1, d) for m in jnp.split(mod, N_MOD, axis=-1)]

        w = w_in[l]
        wa = w[:, :o_a].astype(BF16)
        wq3 = w[:, o_q:o_k].T.reshape(FOX_HEADS, HEAD_DIM, d)
        wqt = jnp.pad(wq3, ((0, 0), (0, pad_head), (0, 0))).reshape(FOX_HEADS * LANES, d).astype(BF16)
        wk3 = w[:, o_k:o_v].reshape(d, FOX_HEADS, HEAD_DIM)
        wk = jnp.pad(wk3, ((0, 0), (0, 0), (0, pad_head))).reshape(d, FOX_HEADS * LANES).astype(BF16)
        wvt = w[:, o_v:o_b].T.astype(BF16)
        wf = jnp.pad(w[:, o_b:o_f], ((0, 0), (0, LANES - FOX_HEADS)))
        wg = w[:, o_f:].astype(BF16)
        qa, ka, va, qt, kpad, vt, fl, ga, gb = _inproj(
            xc, norm_mix_g[l].reshape(1, d), sc1, sh1, wa, wqt, wk, wvt, wf, wg, seq, tm)

        bpad = jnp.pad(b_forget[l], (0, LANES - FOX_HEADS)).reshape(1, LANES)
        kf = _kf(fl.reshape(bsz, seq, LANES), bpad, kpad, place)

        oa = _swa(qa, ka, va, sinks[l], bias, bsz, seq)
        ob = _fox(qt, kf, vt, bsz, seq, _row_tile(seq, 256))

        x1, h2t, q = _mix(xc, oa, ob, ga, gb, gt1, norm_ffn_g[l].reshape(1, d), sc2, sh2,
                          w_branch_a[l].astype(BF16), w_branch_b[l].astype(BF16),
                          w_out[l].astype(BF16), w_query[l].astype(BF16), seq, tm)

        e2, e1, tau = _route(q, sub_keys[l], _row_tile(seq, 256))
        ni = 8
        peer_t = _experts(h2t, expert_down[l].astype(BF16), expert_up[l].T.astype(BF16),
                          e2, e1, tau, tm, ni)

        g_next = final_norm_g.reshape(1, d)
        xc = _final(x1, peer_t, gt2, g_next, seq, tm)
        assert depth == 1
    return xc.reshape(bsz, seq, d)
```

```python
import functools
import math

import numpy as np
import jax
import jax.numpy as jnp
from jax import lax
from jax.experimental import pallas as pl
from jax.experimental.pallas import tpu as pltpu

HEAD_DIM = 64
SWA_Q_HEADS = 8
SWA_KV_HEADS = 2
SWA_GROUP = SWA_Q_HEADS // SWA_KV_HEADS
WINDOW = 128
BLOCK = 128
FOX_HEADS = 8
SWA_WIDTH = SWA_Q_HEADS * HEAD_DIM
SWA_KV_WIDTH = SWA_KV_HEADS * HEAD_DIM
FOX_WIDTH = FOX_HEADS * HEAD_DIM
NUM_BUCKETS = 32
MAX_DISTANCE = 128
PEER_HEADS = 8
PEER_TOPK = 16
PEER_HALF = 64
N_MOD = 6
EPS = 1e-6
NEG = -1e30
LANES = 128
BF16_SUBLANES = 16
VMEM_LIMIT = 56 * 1024 * 1024
LOG2E = math.log2(math.e)
N_FSPLIT = 3
FOX_TK = 256
MXU_PIECE_ROWS = 512
MXU_PIECE_COLS = 256

F32 = jnp.float32
BF16 = jnp.bfloat16


def _cparams(sem):
    return pltpu.CompilerParams(dimension_semantics=sem, vmem_limit_bytes=VMEM_LIMIT)


def _dot(a, b):
    return jnp.dot(a, b, preferred_element_type=F32)


def _dot_nt(a, b):
    return lax.dot_general(a, b, (((1,), (1,)), ((), ())), preferred_element_type=F32)


def _rms_mod(x, g, sc, sh):
    ms = jnp.mean(x * x, axis=-1, keepdims=True)
    return (x * lax.rsqrt(ms + EPS) * g) * (1.0 + sc) + sh


def _ada_kernel(c_ref, w_ref, b_ref, o_ref):
    c = c_ref[...]
    ca = c * jax.nn.sigmoid(c)
    o_ref[...] = _dot(ca, w_ref[...]) + b_ref[...]


def _ada(c, w_ada, b_ada):
    bsz, d = c.shape
    n = w_ada.shape[1]
    tn = d
    return pl.pallas_call(
        _ada_kernel,
        grid=(n // tn,),
        in_specs=[pl.BlockSpec((bsz, d), lambda j: (0, 0)),
                  pl.BlockSpec((d, tn), lambda j: (0, j)),
                  pl.BlockSpec((1, tn), lambda j: (0, j))],
        out_specs=pl.BlockSpec((bsz, tn), lambda j: (0, j)),
        out_shape=jax.ShapeDtypeStruct((bsz, n), F32),
        compiler_params=_cparams(("arbitrary",)),
        name="ada",
    )(c, w_ada, b_ada.reshape(1, n))


def _inproj_kernel(x_ref, g_ref, sc_ref, sh_ref, wa_ref, wqt_ref, wk_ref, wvt_ref, wf_ref, wg_ref,
                   qa_ref, ka_ref, va_ref, qt_ref, kp_ref, vt_ref, fl_ref, ga_ref, gb_ref):
    h = _rms_mod(x_ref[...], g_ref[...], sc_ref[0], sh_ref[0])
    hb = h.astype(BF16)
    ht = h.T.astype(BF16)
    scale = HEAD_DIM ** -0.5
    pa = _dot(hb, wa_ref[...])
    qa_ref[...] = (pa[:, :SWA_WIDTH] * scale).astype(BF16)
    ka_ref[...] = pa[:, SWA_WIDTH:SWA_WIDTH + SWA_KV_WIDTH].astype(BF16)
    va_ref[...] = pa[:, SWA_WIDTH + SWA_KV_WIDTH:].astype(BF16)
    qt = _dot(wqt_ref[...], ht) * (scale * LOG2E)
    r = lax.broadcasted_iota(jnp.int32, qt.shape, 0) % LANES
    qt = jnp.where((r >= HEAD_DIM) & (r < HEAD_DIM + N_FSPLIT), 1.0, qt)
    qt_ref[...] = qt.astype(BF16).reshape(qt_ref.shape)
    kp_ref[...] = _dot(hb, wk_ref[...]).astype(BF16)
    vt = _dot(wvt_ref[...], ht).astype(BF16)
    tk = vt_ref.shape[2]
    for cb in range(vt_ref.shape[0]):
        vt_ref[cb] = vt[:, cb * tk:(cb + 1) * tk]
    fl_ref[...] = _dot(h, wf_ref[...])
    d = ga_ref.shape[1]
    pg = _dot(hb, wg_ref[...])
    ga_ref[...] = jax.nn.sigmoid(pg[:, :d]).astype(BF16)
    gb_ref[...] = jax.nn.sigmoid(pg[:, d:]).astype(BF16)


def _inproj(x2d, g, sc1, sh1, wa, wqt, wk, wvt, wf, wg, seq, tm):
    t, d = x2d.shape
    per_b = seq // tm
    hp = FOX_HEADS * LANES
    row = lambda w: pl.BlockSpec((tm, w), lambda i: (i, 0))
    full = lambda a: pl.BlockSpec(a.shape, lambda i: (0,) * a.ndim)
    mod = pl.BlockSpec((1, 1, d), lambda i: (i // per_b, 0, 0))
    out_shape = [jax.ShapeDtypeStruct((t, SWA_WIDTH), BF16),
                 jax.ShapeDtypeStruct((t, SWA_KV_WIDTH), BF16),
                 jax.ShapeDtypeStruct((t, SWA_KV_WIDTH), BF16),
                 jax.ShapeDtypeStruct((FOX_HEADS, LANES, t), BF16),
                 jax.ShapeDtypeStruct((t, hp), BF16),
                 jax.ShapeDtypeStruct((t // FOX_TK, FOX_WIDTH, FOX_TK), BF16),
                 jax.ShapeDtypeStruct((t, LANES), F32),
                 jax.ShapeDtypeStruct((t, d), BF16), jax.ShapeDtypeStruct((t, d), BF16)]
    out_specs = [row(SWA_WIDTH), row(SWA_KV_WIDTH), row(SWA_KV_WIDTH),
                 pl.BlockSpec((FOX_HEADS, LANES, tm), lambda i: (0, 0, i)),
                 row(hp),
                 pl.BlockSpec((tm // FOX_TK, FOX_WIDTH, FOX_TK), lambda i: (i, 0, 0)),
                 row(LANES), row(d), row(d)]
    return pl.pallas_call(
        _inproj_kernel,
        grid=(t // tm,),
        in_specs=[row(d), full(g), mod, mod, full(wa), full(wqt), full(wk), full(wvt), full(wf), full(wg)],
        out_specs=out_specs,
        out_shape=out_shape,
        compiler_params=_cparams(("parallel",)),
        name="inproj",
    )(x2d, g, sc1, sh1, wa, wqt, wk, wvt, wf, wg)


def _kf_kernel(fl_ref, b_ref, kp_ref, place_ref, ko_ref):
    z = fl_ref[0] + b_ref[...]
    lf = jnp.minimum(z, 0.0) * LOG2E - jnp.log2(1.0 + jnp.exp(-jnp.abs(z)))
    s = lf.shape[0]
    rows = lax.broadcasted_iota(jnp.int32, lf.shape, 0)
    acc = lf
    k = 1
    while k < s:
        shifted = pltpu.roll(acc, k, 0)
        acc = acc + jnp.where(rows >= k, shifted, 0.0)
        k *= 2
    rem = -acc
    placed = None
    for piece in range(N_FSPLIT):
        part = rem.astype(BF16)
        rem = rem - part.astype(F32)
        y = _dot(part, place_ref[piece])
        placed = y if placed is None else placed + y
    ko_ref[...] = (kp_ref[...].astype(F32) + placed).astype(BF16)


def _kf(fl3, bpad, kpad, place):
    bsz, s, _ = fl3.shape
    hp = kpad.shape[1]
    return pl.pallas_call(
        _kf_kernel,
        grid=(bsz,),
        in_specs=[pl.BlockSpec((1, s, LANES), lambda b: (b, 0, 0)),
                  pl.BlockSpec((1, LANES), lambda b: (0, 0)),
                  pl.BlockSpec((s, hp), lambda b: (b, 0)),
                  pl.BlockSpec(place.shape, lambda b: (0, 0, 0))],
        out_specs=pl.BlockSpec((s, hp), lambda b: (b, 0)),
        out_shape=jax.ShapeDtypeStruct(kpad.shape, BF16),
        compiler_params=_cparams(("parallel",)),
        name="kf",
    )(fl3, bpad, kpad, place)


def _swa_kernel(sink_ref, rel_ref, q_ref, kp_ref, kc_ref, vp_ref, vc_ref, bucket_ref, o_ref, bias_ref):
    n = pl.program_id(1)

    @pl.when((pl.program_id(0) == 0) & (n == 0))
    def _():
        bucket = bucket_ref[...]
        for h in range(SWA_Q_HEADS):
            tile = jnp.zeros(bucket.shape, F32)
            for bk in range(NUM_BUCKETS):
                tile = jnp.where(bucket == bk, rel_ref[bk, h], tile)
            bias_ref[h] = tile

    qi = lax.broadcasted_iota(jnp.int32, (BLOCK, 2 * BLOCK), 0)
    kj = lax.broadcasted_iota(jnp.int32, (BLOCK, 2 * BLOCK), 1)
    dist = qi + BLOCK - kj
    valid = (dist >= 0) & (dist < WINDOW) & ((kj >= BLOCK) | (n > 0))
    q = q_ref[...]
    kband = jnp.concatenate([kp_ref[...], kc_ref[...]], axis=0)
    vband = jnp.concatenate([vp_ref[...], vc_ref[...]], axis=0)
    outs = []
    for h in range(SWA_Q_HEADS):
        kh = h // SWA_GROUP
        qh = q[:, h * HEAD_DIM:(h + 1) * HEAD_DIM]
        kk = kband[:, kh * HEAD_DIM:(kh + 1) * HEAD_DIM]
        vv = vband[:, kh * HEAD_DIM:(kh + 1) * HEAD_DIM]
        s = _dot_nt(qh, kk) + bias_ref[h]
        s = jnp.where(valid, s, NEG)
        sink = sink_ref[h]
        m = jnp.maximum(jnp.max(s, axis=-1, keepdims=True), sink)
        p = jnp.exp(s - m)
        denom = jnp.sum(p, axis=-1, keepdims=True) + jnp.exp(sink - m)
        o = _dot(p.astype(BF16), vv)
        outs.append(o / denom)
    o_ref[...] = jnp.concatenate(outs, axis=-1).astype(o_ref.dtype)


def _swa(qa, ka, va, sinks, rel_bias, bucket, bsz, seq):
    nb = seq // BLOCK
    cur = lambda b, n: (b * nb + n, 0)
    prev = lambda b, n: (b * nb + jnp.maximum(n - 1, 0), 0)
    return pl.pallas_call(
        _swa_kernel,
        grid=(bsz, nb),
        in_specs=[pl.BlockSpec(memory_space=pltpu.SMEM),
                  pl.BlockSpec(memory_space=pltpu.SMEM),
                  pl.BlockSpec((BLOCK, SWA_WIDTH), cur),
                  pl.BlockSpec((BLOCK, SWA_KV_WIDTH), prev),
                  pl.BlockSpec((BLOCK, SWA_KV_WIDTH), cur),
                  pl.BlockSpec((BLOCK, SWA_KV_WIDTH), prev),
                  pl.BlockSpec((BLOCK, SWA_KV_WIDTH), cur),
                  pl.BlockSpec(bucket.shape, lambda b, n: (0, 0))],
        out_specs=pl.BlockSpec((BLOCK, SWA_WIDTH), cur),
        out_shape=jax.ShapeDtypeStruct(qa.shape, BF16),
        scratch_shapes=[pltpu.VMEM((SWA_Q_HEADS, BLOCK, 2 * BLOCK), F32)],
        compiler_params=_cparams(("arbitrary", "arbitrary")),
        name="swa",
    )(sinks, rel_bias, qa, ka, ka, va, va, bucket)


def _fox_kernel(q_ref, k_ref, v_ref, o_ref, m_ref, l_ref, acc_ref, *, tq, tk):
    qi = pl.program_id(1)
    m_ref[...] = jnp.full(m_ref.shape, NEG, F32)
    l_ref[...] = jnp.zeros(l_ref.shape, F32)
    acc_ref[...] = jnp.zeros(acc_ref.shape, F32)
    per_q = tq // tk

    def block(j, masked):
        krow0 = pl.multiple_of(j * tk, tk)
        if masked:
            kpos = j * tk + lax.broadcasted_iota(jnp.int32, (tk, tq), 0)
            qpos = qi * tq + lax.broadcasted_iota(jnp.int32, (tk, tq), 1)
            causal = kpos <= qpos

        def scores(h):
            kblk = k_ref[pl.ds(krow0, tk), h * LANES:(h + 1) * LANES]
            return _dot(kblk, q_ref[h])

        s_next = scores(0)
        for h in range(FOX_HEADS):
            s = s_next
            if h + 1 < FOX_HEADS:
                s_next = scores(h + 1)
            if masked:
                s = jnp.where(causal, s, NEG)
            m_old = m_ref[h:h + 1, :]
            m_new = jnp.maximum(m_old, jnp.max(s, axis=0, keepdims=True))
            alpha = jnp.exp2(m_old - m_new)
            p = jnp.exp2(s - m_new)
            l_ref[h:h + 1, :] = alpha * l_ref[h:h + 1, :] + jnp.sum(p, axis=0, keepdims=True)
            m_ref[h:h + 1, :] = m_new
            rows = slice(h * HEAD_DIM, (h + 1) * HEAD_DIM)
            pv = _dot(v_ref[j, rows, :], p.astype(BF16))
            acc_ref[rows, :] = alpha * acc_ref[rows, :] + pv

    def body(j, carry):
        block(j, False)
        return carry

    lax.fori_loop(0, qi * per_q, body, 0)
    for jj in range(per_q):
        block(qi * per_q + jj, True)

    outs = []
    for h in range(FOX_HEADS):
        rows = slice(h * HEAD_DIM, (h + 1) * HEAD_DIM)
        outs.append(acc_ref[rows, :] / l_ref[h:h + 1, :])
    o_ref[...] = jnp.concatenate(outs, axis=0).T.astype(o_ref.dtype)


def _fox(qt, kf, vt, bsz, seq, tq):
    tk = FOX_TK
    nq = seq // tq
    hp = kf.shape[1]
    return pl.pallas_call(
        functools.partial(_fox_kernel, tq=tq, tk=tk),
        grid=(bsz, nq),
        in_specs=[pl.BlockSpec((FOX_HEADS, LANES, tq), lambda b, i: (0, 0, b * nq + i)),
                  pl.BlockSpec((seq, hp), lambda b, i: (b, 0)),
                  pl.BlockSpec((seq // tk, FOX_WIDTH, tk), lambda b, i: (b, 0, 0))],
        out_specs=pl.BlockSpec((tq, FOX_WIDTH), lambda b, i: (b * nq + i, 0)),
        out_shape=jax.ShapeDtypeStruct((bsz * seq, FOX_WIDTH), BF16),
        scratch_shapes=[pltpu.VMEM((FOX_HEADS, tq), F32),
                        pltpu.VMEM((FOX_HEADS, tq), F32),
                        pltpu.VMEM((FOX_WIDTH, tq), F32)],
        compiler_params=_cparams(("parallel", "arbitrary")),
        name="fox",
    )(qt, kf, vt)


def _mix_kernel(x_ref, oa_ref, ob_ref, ga_ref, gb_ref, gt1_ref, g2_ref, sc2_ref, sh2_ref,
                wa_ref, wb_ref, wo_ref, wq_ref, x1_ref, h2t_ref, q_ref):
    ya = _dot(oa_ref[...], wa_ref[...])
    yb = _dot(ob_ref[...], wb_ref[...])
    mixed = ga_ref[...].astype(F32) * ya + gb_ref[...].astype(F32) * yb
    x1 = x_ref[...] + gt1_ref[0] * _dot(mixed.astype(BF16), wo_ref[...])
    x1_ref[...] = x1
    h2 = _rms_mod(x1, g2_ref[...], sc2_ref[0], sh2_ref[0])
    h2t_ref[...] = h2.T.astype(BF16)
    q_ref[...] = _dot(h2.astype(BF16), wq_ref[...])


def _mix(x2d, oa, ob, ga, gb, gt1, g2, sc2, sh2, wa, wb, wo, wq, seq, tm):
    t, d = x2d.shape
    per_b = seq // tm
    row = lambda w: pl.BlockSpec((tm, w), lambda i: (i, 0))
    full = lambda a: pl.BlockSpec(a.shape, lambda i: (0,) * a.ndim)
    mod = pl.BlockSpec((1, 1, d), lambda i: (i // per_b, 0, 0))
    return pl.pallas_call(
        _mix_kernel,
        grid=(t // tm,),
        in_specs=[row(d), row(SWA_WIDTH), row(FOX_WIDTH), row(d), row(d), mod, full(g2), mod, mod,
                  full(wa), full(wb), full(wo), full(wq)],
        out_specs=[row(d), pl.BlockSpec((d, tm), lambda i: (0, i)), row(wq.shape[1])],
        out_shape=[jax.ShapeDtypeStruct((t, d), F32), jax.ShapeDtypeStruct((d, t), BF16),
                   jax.ShapeDtypeStruct((t, wq.shape[1]), F32)],
        compiler_params=_cparams(("parallel",)),
        name="mix",
    )(x2d, oa, ob, ga, gb, gt1, g2, sc2, sh2, wa, wb, wo, wq)


def _top_desc(vals, count):
    rowidx = lax.broadcasted_iota(jnp.int32, vals.shape, 0).astype(F32)
    tops = []
    work = vals
    for _ in range(count):
        mx = jnp.max(work, axis=0, keepdims=True)
        first = jnp.min(jnp.where(work == mx, rowidx, float(vals.shape[0])), axis=0, keepdims=True)
        tops.append(mx)
        work = jnp.where(rowidx == first, -jnp.inf, work)
    return tops, work


def _sort_network(n):
    def merge(lo, hi, r):
        step = r * 2
        if step < hi - lo:
            yield from merge(lo, hi, step)
            yield from merge(lo + r, hi, step)
            yield from [(i, i + r) for i in range(lo + r, hi - r, step)]
        else:
            yield (lo, lo + r)

    def sort(lo, hi):
        if hi - lo >= 1:
            mid = lo + (hi - lo) // 2
            yield from sort(lo, mid)
            yield from sort(mid + 1, hi)
            yield from merge(lo, hi, 1)

    return list(sort(0, n - 1))


def _top_desc_sorted(vals, count):
    n = vals.shape[0]
    depth = n // 8
    lists = [vals[8 * r:8 * (r + 1), :] for r in range(depth)]
    for a, b in _sort_network(depth):
        hi, lo = jnp.maximum(lists[a], lists[b]), jnp.minimum(lists[a], lists[b])
        lists[a], lists[b] = hi, lo
    tops = []
    for k in range(count):
        mx = jnp.max(lists[0], axis=0, keepdims=True)
        tops.append(mx)
        need = min(depth, count - 1 - k)
        if need == 0:
            break
        pop = lists[0] == mx
        for r in range(need):
            nxt = lists[r + 1] if r + 1 < depth else -jnp.inf
            lists[r] = jnp.where(pop, nxt, lists[r])
    return tops


def _route_kernel(q_ref, sk_ref, e2_ref, r2_ref, e1_ref, rk_ref):
    q = q_ref[...]
    nk = sk_ref.shape[2]
    sorted_ok = nk % 8 == 0 and (nk // 8) & (nk // 8 - 1) == 0
    top = _top_desc_sorted if sorted_ok else (lambda v, n: _top_desc(v, n)[0])
    k = PEER_TOPK
    pairs = [(r1, r2) for r1 in range(k) for r2 in range(k) if (r1 + 1) * (r2 + 1) <= k]
    for h in range(PEER_HEADS):
        q1 = q[:, (2 * h) * PEER_HALF:(2 * h + 1) * PEER_HALF]
        q2 = q[:, (2 * h + 1) * PEER_HALF:(2 * h + 2) * PEER_HALF]
        s1 = _dot_nt(sk_ref[h, 0], q1)
        s2 = _dot_nt(sk_ref[h, 1], q2)
        a = top(s1, k)
        b = top(s2, k)
        cands = [a[r1] + b[r2] for r1, r2 in pairs]
        cands += [jnp.full_like(cands[0], -jnp.inf)] * ((-len(cands)) % 8)
        cand = jnp.concatenate(cands, axis=0)
        c, left = _top_desc(cand, k)
        z = jnp.zeros_like(c[0])
        for r in range(k):
            z = z + jnp.exp(c[r] - c[0])
        taken = jnp.where((left == -jnp.inf) & (cand != -jnp.inf), 1.0, 0.0)
        n2 = [jnp.zeros_like(z) for _ in range(k)]
        for row, (r1, r2) in enumerate(pairs):
            n2[r2] = n2[r2] + taken[row:row + 1, :]
        rk = jnp.zeros(s1.shape, F32)
        r2 = jnp.zeros(s2.shape, F32)
        for r in range(k):
            rk = jnp.where(a[r] > s1, float(r + 1), rk)
        for r in reversed(range(k)):
            r2 = jnp.where(s2 == b[r], n2[r], r2)
        e2_ref[h] = pltpu.bitcast(jnp.exp(s2 - b[0]).astype(BF16), jnp.uint32)
        r2_ref[h] = pltpu.bitcast(r2.astype(BF16), jnp.uint32)
        e1_ref[h] = jnp.exp(s1 - a[0]) * (0.5 / z)
        rk_ref[h] = rk


def _route(q, sub_keys, tr):
    t, w = q.shape
    nk = sub_keys.shape[2]
    spec = pl.BlockSpec((PEER_HEADS, nk, tr), lambda i: (0, 0, i))
    shp = jax.ShapeDtypeStruct((PEER_HEADS, nk, t), F32)
    spec_j = pl.BlockSpec((PEER_HEADS, nk // 2, tr), lambda i: (0, 0, i))
    shp_j = jax.ShapeDtypeStruct((PEER_HEADS, nk // 2, t), jnp.uint32)
    return pl.pallas_call(
        _route_kernel,
        grid=(t // tr,),
        in_specs=[pl.BlockSpec((tr, w), lambda i: (i, 0)),
                  pl.BlockSpec(sub_keys.shape, lambda i: (0, 0, 0, 0))],
        out_specs=[spec_j, spec_j, spec, spec],
        out_shape=[shp_j, shp_j, shp, shp],
        compiler_params=_cparams(("parallel",)),
        name="route",
    )(q, sub_keys)


def _experts_kernel(h2t_ref, dn_ref, upt_ref, e2_ref, r2_ref, e1_ref, rk_ref, o_ref, act_ref, g_ref,
                    *, nk, ni, tq, ne, n_items):
    s = pl.program_id(0)
    c_item = jnp.clip(s - 2, 0, n_items - 1)

    @pl.when(s == 0)
    def _():
        act_ref[...] = jnp.zeros(act_ref.shape, F32)
        g_ref[...] = jnp.zeros(g_ref.shape, BF16)

    @pl.when(c_item % ne == 0)
    def _():
        o_ref[...] = jnp.zeros(o_ref.shape, F32)

    slot_a = s % 2
    slot_b = (s + 1) % 2
    et = ni * nk
    d = o_ref.shape[0]
    inv_sqrt2 = 2.0 ** -0.5
    mrow = min(MXU_PIECE_ROWS, et, d)
    ncol = min(MXU_PIECE_COLS, tq)

    def piece_a(mr, nc):
        rows, cols = slice(mr * mrow, (mr + 1) * mrow), slice(nc * ncol, (nc + 1) * ncol)
        words = slice(mr * mrow // 2, (mr + 1) * mrow // 2)
        act_ref[slot_a, rows, cols] = _dot(pltpu.bitcast(dn_ref[words, :], BF16), h2t_ref[:, cols])

    def piece_c(mr, nc):
        rows, cols = slice(mr * mrow, (mr + 1) * mrow), slice(nc * ncol, (nc + 1) * ncol)
        words = slice(mr * mrow // 2, (mr + 1) * mrow // 2)
        o_ref[rows, cols] += _dot(pltpu.bitcast(upt_ref[words, :], BF16), g_ref[slot_a, :, cols])

    def block_b(il, c):
        rows, cols = slice(il * nk, (il + 1) * nk), slice(c * LANES, (c + 1) * LANES)
        sub = BF16_SUBLANES
        w = [None] * (nk // sub)
        for h in range(PEER_HEADS):
            e1b = jnp.broadcast_to(e1_ref[h, il:il + 1, cols], (sub, LANES)).astype(BF16)
            rkb = jnp.broadcast_to(rk_ref[h, il:il + 1, cols], (sub, LANES)).astype(BF16)
            for k in range(nk // sub):
                jr = slice(k * sub, (k + 1) * sub)
                wr = slice(k * sub // 2, (k + 1) * sub // 2)
                p = pltpu.bitcast(e2_ref[h, wr, cols], BF16) * e1b
                sel = jnp.where(rkb < pltpu.bitcast(r2_ref[h, wr, cols], BF16), p, jnp.zeros_like(p))
                w[k] = sel if w[k] is None else w[k] + sel
        for k in range(nk // sub):
            jr = slice(il * nk + k * sub, il * nk + (k + 1) * sub)
            x = act_ref[slot_b, jr, cols]
            g_ref[slot_b, jr, cols] = (x * (1.0 + lax.erf(x * inv_sqrt2))).astype(BF16) * w[k]

    pieces = []
    for nc in range(tq // ncol):
        for mr in range(max(et, d) // mrow):
            if mr < et // mrow:
                pieces.append(functools.partial(piece_a, mr, nc))
            if mr < d // mrow:
                pieces.append(functools.partial(piece_c, mr, nc))
    blocks = [(il, c) for il in range(ni) for c in range(tq // LANES)]
    lead = min(2, len(pieces))
    for k in range(lead):
        pieces[k]()
    per_piece = max(1, len(blocks) // max(1, len(pieces) - lead + 1))
    nxt = lead
    for k, (il, c) in enumerate(blocks):
        block_b(il, c)
        if (k + 1) % per_piece == 0 and nxt < len(pieces):
            pieces[nxt]()
            nxt += 1
    for k in range(nxt, len(pieces)):
        pieces[k]()


def _experts(h2t, dn, upt, e2, r2, e1, rk, tq, ni):
    d, t = h2t.shape
    n_exp = dn.shape[0] * 2
    nk = e1.shape[1]
    et = ni * nk
    ne = n_exp // et
    n_items = (t // tq) * ne
    a_item = lambda s: jnp.minimum(s, n_items - 1)
    b_item = lambda s: jnp.clip(s - 1, 0, n_items - 1)
    c_item = lambda s: jnp.clip(s - 2, 0, n_items - 1)
    per_j = pl.BlockSpec((PEER_HEADS, nk // 2, tq), lambda s: (0, 0, b_item(s) // ne))
    per_i = pl.BlockSpec((PEER_HEADS, ni, tq), lambda s: (0, b_item(s) % ne, b_item(s) // ne))
    return pl.pallas_call(
        functools.partial(_experts_kernel, nk=nk, ni=ni, tq=tq, ne=ne, n_items=n_items),
        grid=(n_items + 2,),
        in_specs=[pl.BlockSpec((d, tq), lambda s: (0, a_item(s) // ne)),
                  pl.BlockSpec((et // 2, d), lambda s: (a_item(s) % ne, 0)),
                  pl.BlockSpec((d // 2, et), lambda s: (0, c_item(s) % ne)),
                  per_j, per_j, per_i, per_i],
        out_specs=pl.BlockSpec((d, tq), lambda s: (0, c_item(s) // ne)),
        out_shape=jax.ShapeDtypeStruct((d, t), F32),
        scratch_shapes=[pltpu.VMEM((2, et, tq), F32), pltpu.VMEM((2, et, tq), BF16)],
        compiler_params=_cparams(("arbitrary",)),
        name="experts",
    )(h2t, dn, upt, e2, r2, e1, rk)


def _final_kernel(x1_ref, pt_ref, gt2_ref, g_ref, o_ref):
    x2 = x1_ref[...] + gt2_ref[0] * pt_ref[...].T
    ms = jnp.mean(x2 * x2, axis=-1, keepdims=True)
    o_ref[...] = x2 * lax.rsqrt(ms + EPS) * g_ref[...]


def _final(x1, peer_t, gt2, g, seq, tm):
    t, d = x1.shape
    per_b = seq // tm
    return pl.pallas_call(
        _final_kernel,
        grid=(t // tm,),
        in_specs=[pl.BlockSpec((tm, d), lambda i: (i, 0)),
                  pl.BlockSpec((d, tm), lambda i: (0, i)),
                  pl.BlockSpec((1, 1, d), lambda i: (i // per_b, 0, 0)),
                  pl.BlockSpec((1, d), lambda i: (0, 0))],
        out_specs=pl.BlockSpec((tm, d), lambda i: (i, 0)),
        out_shape=jax.ShapeDtypeStruct((t, d), F32),
        compiler_params=_cparams(("parallel",)),
        name="final",
    )(x1, peer_t, gt2, g)


def _t5_bucket(dist):
    dist = np.clip(dist, 0, None)
    max_exact = NUM_BUCKETS // 2
    large = max_exact + (np.log(np.maximum(dist, 1) / max_exact) / np.log(MAX_DISTANCE / max_exact)
                         * (NUM_BUCKETS - max_exact)).astype(np.int32)
    large = np.minimum(large, NUM_BUCKETS - 1)
    return np.where(dist < max_exact, dist, large).astype(np.int32)


def _pack_kernel(x_ref, o_ref, *, transpose):
    x = x_ref[...]
    if transpose:
        x = x.T
    o_ref[...] = pltpu.bitcast(x.astype(BF16), jnp.uint32)


def _pack_bf16_rows(x, transpose=False, tile=512):
    r, c = x.shape
    if transpose:
        in_spec = pl.BlockSpec((tile, c), lambda i: (i, 0))
        out_spec = pl.BlockSpec((c // 2, tile), lambda i: (0, i))
        out_shape = jax.ShapeDtypeStruct((c // 2, r), jnp.uint32)
    else:
        in_spec = pl.BlockSpec((tile, c), lambda i: (i, 0))
        out_spec = pl.BlockSpec((tile // 2, c), lambda i: (i, 0))
        out_shape = jax.ShapeDtypeStruct((r // 2, c), jnp.uint32)
    return pl.pallas_call(
        functools.partial(_pack_kernel, transpose=transpose),
        grid=(r // tile,),
        in_specs=[in_spec],
        out_specs=out_spec,
        out_shape=out_shape,
        compiler_params=_cparams(("parallel",)),
        name="pack_t" if transpose else "pack",
    )(x)


def _row_tile(seq, want):
    tm = min(want, seq)
    assert seq % tm == 0
    return tm


def _fsplit_placement():
    place = np.zeros((N_FSPLIT, LANES, FOX_HEADS * LANES), np.float32)
    for p in range(N_FSPLIT):
        for h in range(FOX_HEADS):
            place[p, h, h * LANES + HEAD_DIM + p] = 1.0
    return jnp.asarray(place, BF16)


def kernel(x, c, norm_mix_g, norm_ffn_g, w_ada, b_ada, w_in, b_forget, sinks, w_branch_a,
           w_branch_b, w_out, rel_bias, w_query, sub_keys, expert_down, expert_up, final_norm_g):
    bsz, seq, d = x.shape
    depth = w_ada.shape[0]
    t = bsz * seq
    tm = _row_tile(seq, 512)
    assert seq % FOX_TK == 0 and tm % FOX_TK == 0

    dist = np.arange(BLOCK)[:, None] + BLOCK - np.arange(2 * BLOCK)[None, :]
    bucket = jnp.asarray(_t5_bucket(dist), jnp.int32)
    place = _fsplit_placement()

    o_a = SWA_WIDTH + 2 * SWA_KV_WIDTH
    o_q, o_k, o_v = o_a, o_a + FOX_WIDTH, o_a + 2 * FOX_WIDTH
    o_b = o_a + 3 * FOX_WIDTH
    o_f = o_b + FOX_HEADS
    pad_head = LANES - HEAD_DIM

    xc = x.reshape(t, d)
    for l in range(depth):
        mod = _ada(c, w_ada[l], b_ada[l])
        sh1, sc1, gt1, sh2, sc2, gt2 = [m.reshape(bsz, 1, d) for m in jnp.split(mod, N_MOD, axis=-1)]

        w = w_in[l]
        wa = w[:, :o_a].astype(BF16)
        wq3 = w[:, o_q:o_k].T.reshape(FOX_HEADS, HEAD_DIM, d)
        wqt = jnp.pad(wq3, ((0, 0), (0, pad_head), (0, 0))).reshape(FOX_HEADS * LANES, d).astype(BF16)
        wk3 = w[:, o_k:o_v].reshape(d, FOX_HEADS, HEAD_DIM)
        wk = jnp.pad(wk3, ((0, 0), (0, 0), (0, pad_head))).reshape(d, FOX_HEADS * LANES).astype(BF16)
        wvt = w[:, o_v:o_b].T.astype(BF16)
        wf = jnp.pad(w[:, o_b:o_f], ((0, 0), (0, LANES - FOX_HEADS)))
        wg = w[:, o_f:].astype(BF16)
        qa, ka, va, qt, kpad, vt, fl, ga, gb = _inproj(
            xc, norm_mix_g[l].reshape(1, d), sc1, sh1, wa, wqt, wk, wvt, wf, wg, seq, tm)

        bpad = jnp.pad(b_forget[l], (0, LANES - FOX_HEADS)).reshape(1, LANES)
        kf = _kf(fl.reshape(bsz, seq, LANES), bpad, kpad, place)

        oa = _swa(qa, ka, va, sinks[l], rel_bias.astype(F32), bucket, bsz, seq)
        ob = _fox(qt, kf, vt, bsz, seq, _row_tile(seq, 512))

        x1, h2t, q = _mix(xc, oa, ob, ga, gb, gt1, norm_ffn_g[l].reshape(1, d), sc2, sh2,
                          w_branch_a[l].astype(BF16), w_branch_b[l].astype(BF16),
                          w_out[l].astype(BF16), w_query[l].astype(BF16), seq, tm)

        e2, r2, e1, rk = _route(q, sub_keys[l], _row_tile(seq, 256))
        ni = 8
        peer_t = _experts(h2t, _pack_bf16_rows(expert_down[l]), _pack_bf16_rows(expert_up[l], transpose=True),
                          e2, r2, e1, rk, tm, ni)

        g_next = final_norm_g.reshape(1, d)
        xc = _final(x1, peer_t, gt2, g_next, seq, tm)
        assert depth == 1
    return xc.reshape(bsz, seq, d)
```

```python
import functools
import math

import numpy as np
import jax
import jax.numpy as jnp
from jax import lax
from jax.experimental import pallas as pl
from jax.experimental.pallas import tpu as pltpu

HEAD_DIM = 64
SWA_Q_HEADS = 8
SWA_KV_HEADS = 2
SWA_GROUP = SWA_Q_HEADS // SWA_KV_HEADS
WINDOW = 128
BLOCK = 128
FOX_HEADS = 8
SWA_WIDTH = SWA_Q_HEADS * HEAD_DIM
SWA_KV_WIDTH = SWA_KV_HEADS * HEAD_DIM
FOX_WIDTH = FOX_HEADS * HEAD_DIM
NUM_BUCKETS = 32
MAX_DISTANCE = 128
PEER_HEADS = 8
PEER_TOPK = 16
PEER_HALF = 64
N_MOD = 6
EPS = 1e-6
NEG = -1e30
LANES = 128
BF16_SUBLANES = 16
VMEM_LIMIT = 56 * 1024 * 1024
LOG2E = math.log2(math.e)
N_FSPLIT = 3
FOX_TK = 256
MXU_PIECE_ROWS = 512
MXU_PIECE_COLS = 256
EXPERT_TQ = 1024
IL_GROUP = 1

F32 = jnp.float32
BF16 = jnp.bfloat16


def _cparams(sem):
    return pltpu.CompilerParams(dimension_semantics=sem, vmem_limit_bytes=VMEM_LIMIT)


def _dot(a, b):
    return jnp.dot(a, b, preferred_element_type=F32)


def _dot_nt(a, b):
    return lax.dot_general(a, b, (((1,), (1,)), ((), ())), preferred_element_type=F32)


def _rms_mod(x, g, sc, sh):
    ms = jnp.mean(x * x, axis=-1, keepdims=True)
    return (x * lax.rsqrt(ms + EPS) * g) * (1.0 + sc) + sh


def _ada_kernel(c_ref, w_ref, b_ref, o_ref):
    c = c_ref[...]
    ca = c * jax.nn.sigmoid(c)
    o_ref[...] = _dot(ca, w_ref[...]) + b_ref[...]


def _ada(c, w_ada, b_ada):
    bsz, d = c.shape
    n = w_ada.shape[1]
    tn = d
    return pl.pallas_call(
        _ada_kernel,
        grid=(n // tn,),
        in_specs=[pl.BlockSpec((bsz, d), lambda j: (0, 0)),
                  pl.BlockSpec((d, tn), lambda j: (0, j)),
                  pl.BlockSpec((1, tn), lambda j: (0, j))],
        out_specs=pl.BlockSpec((bsz, tn), lambda j: (0, j)),
        out_shape=jax.ShapeDtypeStruct((bsz, n), F32),
        compiler_params=_cparams(("arbitrary",)),
        name="ada",
    )(c, w_ada, b_ada.reshape(1, n))


def _inproj_kernel(x_ref, g_ref, sc_ref, sh_ref, wa_ref, wqt_ref, wk_ref, wvt_ref, wf_ref, wg_ref,
                   qa_ref, ka_ref, va_ref, qt_ref, kp_ref, vt_ref, fl_ref, ga_ref, gb_ref):
    h = _rms_mod(x_ref[...], g_ref[...], sc_ref[0], sh_ref[0])
    hb = h.astype(BF16)
    ht = h.T.astype(BF16)
    scale = HEAD_DIM ** -0.5
    pa = _dot(hb, wa_ref[...])
    qa_ref[...] = (pa[:, :SWA_WIDTH] * scale).astype(BF16)
    ka_ref[...] = pa[:, SWA_WIDTH:SWA_WIDTH + SWA_KV_WIDTH].astype(BF16)
    va_ref[...] = pa[:, SWA_WIDTH + SWA_KV_WIDTH:].astype(BF16)
    qt = _dot(wqt_ref[...], ht) * (scale * LOG2E)
    r = lax.broadcasted_iota(jnp.int32, qt.shape, 0) % LANES
    qt = jnp.where((r >= HEAD_DIM) & (r < HEAD_DIM + N_FSPLIT), 1.0, qt)
    qt_ref[...] = qt.astype(BF16).reshape(qt_ref.shape)
    kp_ref[...] = _dot(hb, wk_ref[...]).astype(BF16)
    vt = _dot(wvt_ref[...], ht).astype(BF16)
    tk = vt_ref.shape[2]
    for cb in range(vt_ref.shape[0]):
        vt_ref[cb] = vt[:, cb * tk:(cb + 1) * tk]
    fl_ref[...] = _dot(h, wf_ref[...])
    d = ga_ref.shape[1]
    pg = _dot(hb, wg_ref[...])
    ga_ref[...] = jax.nn.sigmoid(pg[:, :d]).astype(BF16)
    gb_ref[...] = jax.nn.sigmoid(pg[:, d:]).astype(BF16)


def _inproj(x2d, g, sc1, sh1, wa, wqt, wk, wvt, wf, wg, seq, tm):
    t, d = x2d.shape
    per_b = seq // tm
    hp = FOX_HEADS * LANES
    row = lambda w: pl.BlockSpec((tm, w), lambda i: (i, 0))
    full = lambda a: pl.BlockSpec(a.shape, lambda i: (0,) * a.ndim)
    mod = pl.BlockSpec((1, 1, d), lambda i: (i // per_b, 0, 0))
    out_shape = [jax.ShapeDtypeStruct((t, SWA_WIDTH), BF16),
                 jax.ShapeDtypeStruct((t, SWA_KV_WIDTH), BF16),
                 jax.ShapeDtypeStruct((t, SWA_KV_WIDTH), BF16),
                 jax.ShapeDtypeStruct((FOX_HEADS, LANES, t), BF16),
                 jax.ShapeDtypeStruct((t, hp), BF16),
                 jax.ShapeDtypeStruct((t // FOX_TK, FOX_WIDTH, FOX_TK), BF16),
                 jax.ShapeDtypeStruct((t, LANES), F32),
                 jax.ShapeDtypeStruct((t, d), BF16), jax.ShapeDtypeStruct((t, d), BF16)]
    out_specs = [row(SWA_WIDTH), row(SWA_KV_WIDTH), row(SWA_KV_WIDTH),
                 pl.BlockSpec((FOX_HEADS, LANES, tm), lambda i: (0, 0, i)),
                 row(hp),
                 pl.BlockSpec((tm // FOX_TK, FOX_WIDTH, FOX_TK), lambda i: (i, 0, 0)),
                 row(LANES), row(d), row(d)]
    return pl.pallas_call(
        _inproj_kernel,
        grid=(t // tm,),
        in_specs=[row(d), full(g), mod, mod, full(wa), full(wqt), full(wk), full(wvt), full(wf), full(wg)],
        out_specs=out_specs,
        out_shape=out_shape,
        compiler_params=_cparams(("parallel",)),
        name="inproj",
    )(x2d, g, sc1, sh1, wa, wqt, wk, wvt, wf, wg)


def _kf_kernel(fl_ref, b_ref, kp_ref, place_ref, ko_ref):
    z = fl_ref[0] + b_ref[...]
    lf = jnp.minimum(z, 0.0) * LOG2E - jnp.log2(1.0 + jnp.exp(-jnp.abs(z)))
    s = lf.shape[0]
    rows = lax.broadcasted_iota(jnp.int32, lf.shape, 0)
    acc = lf
    k = 1
    while k < s:
        shifted = pltpu.roll(acc, k, 0)
        acc = acc + jnp.where(rows >= k, shifted, 0.0)
        k *= 2
    rem = -acc
    placed = None
    for piece in range(N_FSPLIT):
        part = rem.astype(BF16)
        rem = rem - part.astype(F32)
        y = _dot(part, place_ref[piece])
        placed = y if placed is None else placed + y
    ko_ref[...] = (kp_ref[...].astype(F32) + placed).astype(BF16)


def _kf(fl3, bpad, kpad, place):
    bsz, s, _ = fl3.shape
    hp = kpad.shape[1]
    return pl.pallas_call(
        _kf_kernel,
        grid=(bsz,),
        in_specs=[pl.BlockSpec((1, s, LANES), lambda b: (b, 0, 0)),
                  pl.BlockSpec((1, LANES), lambda b: (0, 0)),
                  pl.BlockSpec((s, hp), lambda b: (b, 0)),
                  pl.BlockSpec(place.shape, lambda b: (0, 0, 0))],
        out_specs=pl.BlockSpec((s, hp), lambda b: (b, 0)),
        out_shape=jax.ShapeDtypeStruct(kpad.shape, BF16),
        compiler_params=_cparams(("parallel",)),
        name="kf",
    )(fl3, bpad, kpad, place)


def _swa_kernel(sink_ref, rel_ref, q_ref, kp_ref, kc_ref, vp_ref, vc_ref, bucket_ref, o_ref, bias_ref):
    n = pl.program_id(1)

    @pl.when((pl.program_id(0) == 0) & (n == 0))
    def _():
        bucket = bucket_ref[...]
        for h in range(SWA_Q_HEADS):
            tile = jnp.zeros(bucket.shape, F32)
            for bk in range(NUM_BUCKETS):
                tile = jnp.where(bucket == bk, rel_ref[bk, h], tile)
            bias_ref[h] = tile

    qi = lax.broadcasted_iota(jnp.int32, (BLOCK, 2 * BLOCK), 0)
    kj = lax.broadcasted_iota(jnp.int32, (BLOCK, 2 * BLOCK), 1)
    dist = qi + BLOCK - kj
    valid = (dist >= 0) & (dist < WINDOW) & ((kj >= BLOCK) | (n > 0))
    q = q_ref[...]
    kband = jnp.concatenate([kp_ref[...], kc_ref[...]], axis=0)
    vband = jnp.concatenate([vp_ref[...], vc_ref[...]], axis=0)
    outs = []
    for h in range(SWA_Q_HEADS):
        kh = h // SWA_GROUP
        qh = q[:, h * HEAD_DIM:(h + 1) * HEAD_DIM]
        kk = kband[:, kh * HEAD_DIM:(kh + 1) * HEAD_DIM]
        vv = vband[:, kh * HEAD_DIM:(kh + 1) * HEAD_DIM]
        s = _dot_nt(qh, kk) + bias_ref[h]
        s = jnp.where(valid, s, NEG)
        sink = sink_ref[h]
        m = jnp.maximum(jnp.max(s, axis=-1, keepdims=True), sink)
        p = jnp.exp(s - m)
        denom = jnp.sum(p, axis=-1, keepdims=True) + jnp.exp(sink - m)
        o = _dot(p.astype(BF16), vv)
        outs.append(o / denom)
    o_ref[...] = jnp.concatenate(outs, axis=-1).astype(o_ref.dtype)


def _swa(qa, ka, va, sinks, rel_bias, bucket, bsz, seq):
    nb = seq // BLOCK
    cur = lambda b, n: (b * nb + n, 0)
    prev = lambda b, n: (b * nb + jnp.maximum(n - 1, 0), 0)
    return pl.pallas_call(
        _swa_kernel,
        grid=(bsz, nb),
        in_specs=[pl.BlockSpec(memory_space=pltpu.SMEM),
                  pl.BlockSpec(memory_space=pltpu.SMEM),
                  pl.BlockSpec((BLOCK, SWA_WIDTH), cur),
                  pl.BlockSpec((BLOCK, SWA_KV_WIDTH), prev),
                  pl.BlockSpec((BLOCK, SWA_KV_WIDTH), cur),
                  pl.BlockSpec((BLOCK, SWA_KV_WIDTH), prev),
                  pl.BlockSpec((BLOCK, SWA_KV_WIDTH), cur),
                  pl.BlockSpec(bucket.shape, lambda b, n: (0, 0))],
        out_specs=pl.BlockSpec((BLOCK, SWA_WIDTH), cur),
        out_shape=jax.ShapeDtypeStruct(qa.shape, BF16),
        scratch_shapes=[pltpu.VMEM((SWA_Q_HEADS, BLOCK, 2 * BLOCK), F32)],
        compiler_params=_cparams(("arbitrary", "arbitrary")),
        name="swa",
    )(sinks, rel_bias, qa, ka, ka, va, va, bucket)


def _fox_kernel(q_ref, k_ref, v_ref, o_ref, m_ref, l_ref, acc_ref, *, tq, tk):
    qi = pl.program_id(1)
    m_ref[...] = jnp.full(m_ref.shape, NEG, F32)
    l_ref[...] = jnp.zeros(l_ref.shape, F32)
    acc_ref[...] = jnp.zeros(acc_ref.shape, F32)
    per_q = tq // tk

    def block(j, masked):
        krow0 = pl.multiple_of(j * tk, tk)
        if masked:
            kpos = j * tk + lax.broadcasted_iota(jnp.int32, (tk, tq), 0)
            qpos = qi * tq + lax.broadcasted_iota(jnp.int32, (tk, tq), 1)
            causal = kpos <= qpos

        def scores(h):
            kblk = k_ref[pl.ds(krow0, tk), h * LANES:(h + 1) * LANES]
            return _dot(kblk, q_ref[h])

        s_next = scores(0)
        for h in range(FOX_HEADS):
            s = s_next
            if h + 1 < FOX_HEADS:
                s_next = scores(h + 1)
            if masked:
                s = jnp.where(causal, s, NEG)
            m_old = m_ref[h:h + 1, :]
            m_new = jnp.maximum(m_old, jnp.max(s, axis=0, keepdims=True))
            alpha = jnp.exp2(m_old - m_new)
            p = jnp.exp2(s - m_new)
            l_ref[h:h + 1, :] = alpha * l_ref[h:h + 1, :] + jnp.sum(p, axis=0, keepdims=True)
            m_ref[h:h + 1, :] = m_new
            rows = slice(h * HEAD_DIM, (h + 1) * HEAD_DIM)
            pv = _dot(v_ref[j, rows, :], p.astype(BF16))
            acc_ref[rows, :] = alpha * acc_ref[rows, :] + pv

    def body(j, carry):
        block(j, False)
        return carry

    lax.fori_loop(0, qi * per_q, body, 0)
    for jj in range(per_q):
        block(qi * per_q + jj, True)

    outs = []
    for h in range(FOX_HEADS):
        rows = slice(h * HEAD_DIM, (h + 1) * HEAD_DIM)
        outs.append(acc_ref[rows, :] / l_ref[h:h + 1, :])
    o_ref[...] = jnp.concatenate(outs, axis=0).T.astype(o_ref.dtype)


def _fox(qt, kf, vt, bsz, seq, tq):
    tk = FOX_TK
    nq = seq // tq
    hp = kf.shape[1]
    return pl.pallas_call(
        functools.partial(_fox_kernel, tq=tq, tk=tk),
        grid=(bsz, nq),
        in_specs=[pl.BlockSpec((FOX_HEADS, LANES, tq), lambda b, i: (0, 0, b * nq + i)),
                  pl.BlockSpec((seq, hp), lambda b, i: (b, 0)),
                  pl.BlockSpec((seq // tk, FOX_WIDTH, tk), lambda b, i: (b, 0, 0))],
        out_specs=pl.BlockSpec((tq, FOX_WIDTH), lambda b, i: (b * nq + i, 0)),
        out_shape=jax.ShapeDtypeStruct((bsz * seq, FOX_WIDTH), BF16),
        scratch_shapes=[pltpu.VMEM((FOX_HEADS, tq), F32),
                        pltpu.VMEM((FOX_HEADS, tq), F32),
                        pltpu.VMEM((FOX_WIDTH, tq), F32)],
        compiler_params=_cparams(("parallel", "arbitrary")),
        name="fox",
    )(qt, kf, vt)


def _mix_kernel(x_ref, oa_ref, ob_ref, ga_ref, gb_ref, gt1_ref, g2_ref, sc2_ref, sh2_ref,
                wa_ref, wb_ref, wo_ref, wq_ref, x1_ref, h2t_ref, q_ref):
    ya = _dot(oa_ref[...], wa_ref[...])
    yb = _dot(ob_ref[...], wb_ref[...])
    mixed = ga_ref[...].astype(F32) * ya + gb_ref[...].astype(F32) * yb
    x1 = x_ref[...] + gt1_ref[0] * _dot(mixed.astype(BF16), wo_ref[...])
    x1_ref[...] = x1
    h2 = _rms_mod(x1, g2_ref[...], sc2_ref[0], sh2_ref[0])
    h2t_ref[...] = h2.T.astype(BF16)
    q_ref[...] = _dot(h2.astype(BF16), wq_ref[...])


def _mix(x2d, oa, ob, ga, gb, gt1, g2, sc2, sh2, wa, wb, wo, wq, seq, tm):
    t, d = x2d.shape
    per_b = seq // tm
    row = lambda w: pl.BlockSpec((tm, w), lambda i: (i, 0))
    full = lambda a: pl.BlockSpec(a.shape, lambda i: (0,) * a.ndim)
    mod = pl.BlockSpec((1, 1, d), lambda i: (i // per_b, 0, 0))
    return pl.pallas_call(
        _mix_kernel,
        grid=(t // tm,),
        in_specs=[row(d), row(SWA_WIDTH), row(FOX_WIDTH), row(d), row(d), mod, full(g2), mod, mod,
                  full(wa), full(wb), full(wo), full(wq)],
        out_specs=[row(d), pl.BlockSpec((d, tm), lambda i: (0, i)), row(wq.shape[1])],
        out_shape=[jax.ShapeDtypeStruct((t, d), F32), jax.ShapeDtypeStruct((d, t), BF16),
                   jax.ShapeDtypeStruct((t, wq.shape[1]), F32)],
        compiler_params=_cparams(("parallel",)),
        name="mix",
    )(x2d, oa, ob, ga, gb, gt1, g2, sc2, sh2, wa, wb, wo, wq)


def _top_desc(vals, count):
    rowidx = lax.broadcasted_iota(jnp.int32, vals.shape, 0).astype(F32)
    tops = []
    work = vals
    for _ in range(count):
        mx = jnp.max(work, axis=0, keepdims=True)
        first = jnp.min(jnp.where(work == mx, rowidx, float(vals.shape[0])), axis=0, keepdims=True)
        tops.append(mx)
        work = jnp.where(rowidx == first, -jnp.inf, work)
    return tops, work


def _sort_network(n):
    def merge(lo, hi, r):
        step = r * 2
        if step < hi - lo:
            yield from merge(lo, hi, step)
            yield from merge(lo + r, hi, step)
            yield from [(i, i + r) for i in range(lo + r, hi - r, step)]
        else:
            yield (lo, lo + r)

    def sort(lo, hi):
        if hi - lo >= 1:
            mid = lo + (hi - lo) // 2
            yield from sort(lo, mid)
            yield from sort(mid + 1, hi)
            yield from merge(lo, hi, 1)

    return list(sort(0, n - 1))


def _top_desc_sorted(vals, count):
    n = vals.shape[0]
    depth = n // 8
    lists = [vals[8 * r:8 * (r + 1), :] for r in range(depth)]
    for a, b in _sort_network(depth):
        hi, lo = jnp.maximum(lists[a], lists[b]), jnp.minimum(lists[a], lists[b])
        lists[a], lists[b] = hi, lo
    tops = []
    for k in range(count):
        mx = jnp.max(lists[0], axis=0, keepdims=True)
        tops.append(mx)
        need = min(depth, count - 1 - k)
        if need == 0:
            break
        pop = lists[0] == mx
        for r in range(need):
            nxt = lists[r + 1] if r + 1 < depth else -jnp.inf
            lists[r] = jnp.where(pop, nxt, lists[r])
    return tops


def _route_kernel(q_ref, sk_ref, e2_ref, r2_ref, e1_ref, rk_ref):
    q = q_ref[...]
    nk = sk_ref.shape[2]
    sorted_ok = nk % 8 == 0 and (nk // 8) & (nk // 8 - 1) == 0
    top = _top_desc_sorted if sorted_ok else (lambda v, n: _top_desc(v, n)[0])
    k = PEER_TOPK
    for h in range(PEER_HEADS):
        q1 = q[:, (2 * h) * PEER_HALF:(2 * h + 1) * PEER_HALF]
        q2 = q[:, (2 * h + 1) * PEER_HALF:(2 * h + 2) * PEER_HALF]
        s1 = _dot_nt(sk_ref[h, 0], q1)
        s2 = _dot_nt(sk_ref[h, 1], q2)
        a = top(s1, k)
        b = top(s2, k)
        c, n1 = _top_pair_sums(a, b, k)
        z = jnp.zeros_like(c[0])
        for r in range(k):
            z = z + jnp.exp(c[r] - c[0])
        rk = jnp.zeros(s2.shape, F32)
        cnt = jnp.zeros(s1.shape, F32)
        for r in range(k):
            rk = jnp.where(b[r] > s2, float(r + 1), rk)
        for r in reversed(range(k)):
            cnt = jnp.where(s1 == a[r], n1[r], cnt)
        e2_ref[h] = pltpu.bitcast(jnp.exp(s2 - b[0]).astype(BF16), jnp.uint32)
        r2_ref[h] = pltpu.bitcast(rk.astype(BF16), jnp.uint32)
        e1_ref[h] = jnp.exp(s1 - a[0]) * (0.5 / z)
        rk_ref[h] = cnt


def _top_pair_sums(a, b, k):
    assert k == 2 * 8, "two sublane groups of lists"
    t = a[0].shape[1]
    a_lo = jnp.concatenate(a[:8], axis=0)
    a_hi = jnp.concatenate(a[8:], axis=0)
    sub = lax.broadcasted_iota(jnp.int32, (8, t), 0)
    idx_lo = sub.astype(F32)
    idx_hi = idx_lo + 8.0
    neg = jnp.full((8, t), -jnp.inf, F32)
    depth = [a_lo + b[0]] + [jnp.where(sub < k // (d + 1), a_lo + b[d], neg) for d in range(1, k)]
    head_hi = a_hi + b[0]
    n_lo = jnp.zeros((8, t), F32)
    n_hi = jnp.zeros((8, t), F32)
    sums = []
    for step in range(k):
        mx = jnp.max(jnp.maximum(depth[0], head_hi), axis=0, keepdims=True)
        sums.append(mx)
        first = jnp.min(jnp.minimum(jnp.where(depth[0] == mx, idx_lo, float(k)),
                                    jnp.where(head_hi == mx, idx_hi, float(k))), axis=0, keepdims=True)
        pop_lo = idx_lo == first
        pop_hi = idx_hi == first
        n_lo = n_lo + jnp.where(pop_lo, 1.0, 0.0)
        n_hi = n_hi + jnp.where(pop_hi, 1.0, 0.0)
        left = k - 1 - step
        for d in range(min(left, k - 1)):
            nxt = depth[d + 1] if d + 1 < k else neg
            depth[d] = jnp.where(pop_lo, nxt, depth[d])
        head_hi = jnp.where(pop_hi, neg, head_hi)
    n1 = [n_lo[r:r + 1, :] for r in range(8)] + [n_hi[r:r + 1, :] for r in range(8)]
    return sums, n1


def _route(q, sub_keys, tr):
    t, w = q.shape
    nk = sub_keys.shape[2]
    spec = pl.BlockSpec((PEER_HEADS, nk, tr), lambda i: (0, 0, i))
    shp = jax.ShapeDtypeStruct((PEER_HEADS, nk, t), F32)
    spec_j = pl.BlockSpec((PEER_HEADS, nk // 2, tr), lambda i: (0, 0, i))
    shp_j = jax.ShapeDtypeStruct((PEER_HEADS, nk // 2, t), jnp.uint32)
    return pl.pallas_call(
        _route_kernel,
        grid=(t // tr,),
        in_specs=[pl.BlockSpec((tr, w), lambda i: (i, 0)),
                  pl.BlockSpec(sub_keys.shape, lambda i: (0, 0, 0, 0))],
        out_specs=[spec_j, spec_j, spec, spec],
        out_shape=[shp_j, shp_j, shp, shp],
        compiler_params=_cparams(("parallel",)),
        name="route",
    )(q, sub_keys)


def _experts_kernel(h2t_ref, dn_ref, upt_ref, e2_ref, r2_ref, e1_ref, rk_ref, o_ref, act_ref, g_ref,
                    *, nk, ni, tq, ne, n_items):
    s = pl.program_id(0)
    c_item = jnp.clip(s - 2, 0, n_items - 1)

    @pl.when(s == 0)
    def _():
        act_ref[...] = jnp.zeros(act_ref.shape, F32)
        g_ref[...] = jnp.zeros(g_ref.shape, BF16)

    @pl.when(c_item % ne == 0)
    def _():
        o_ref[...] = jnp.zeros(o_ref.shape, F32)

    slot_a = s % 2
    slot_b = (s + 1) % 2
    et = ni * nk
    d = o_ref.shape[0]
    inv_sqrt2 = 2.0 ** -0.5
    mrow = min(MXU_PIECE_ROWS, et, d)
    ncol = min(MXU_PIECE_COLS, tq)

    def piece_a(mr, nc):
        rows, cols = slice(mr * mrow, (mr + 1) * mrow), slice(nc * ncol, (nc + 1) * ncol)
        words = slice(mr * mrow // 2, (mr + 1) * mrow // 2)
        act_ref[slot_a, rows, cols] = _dot(pltpu.bitcast(dn_ref[words, :], BF16), h2t_ref[:, cols])

    def piece_c(mr, nc):
        rows, cols = slice(mr * mrow, (mr + 1) * mrow), slice(nc * ncol, (nc + 1) * ncol)
        words = slice(mr * mrow // 2, (mr + 1) * mrow // 2)
        o_ref[rows, cols] += _dot(pltpu.bitcast(upt_ref[words, :], BF16), g_ref[slot_a, :, cols])

    def block_b(il0, c):
        cols = slice(c * LANES, (c + 1) * LANES)
        sub = BF16_SUBLANES
        ils = range(il0, il0 + IL_GROUP)
        w = [[None] * (nk // sub) for _ in ils]
        for h in range(PEER_HEADS):
            e1b = [jnp.broadcast_to(e1_ref[h, il:il + 1, cols], (sub, LANES)).astype(BF16) for il in ils]
            rkb = [jnp.broadcast_to(rk_ref[h, il:il + 1, cols], (sub, LANES)).astype(BF16) for il in ils]
            for k in range(nk // sub):
                wr = slice(k * sub // 2, (k + 1) * sub // 2)
                e2v = pltpu.bitcast(e2_ref[h, wr, cols], BF16)
                r2v = pltpu.bitcast(r2_ref[h, wr, cols], BF16)
                for u in range(IL_GROUP):
                    p = e2v * e1b[u]
                    sel = jnp.where(r2v < rkb[u], p, jnp.zeros_like(p))
                    w[u][k] = sel if w[u][k] is None else w[u][k] + sel
        for u, il in enumerate(ils):
            for k in range(nk // sub):
                jr = slice(il * nk + k * sub, il * nk + (k + 1) * sub)
                x = act_ref[slot_b, jr, cols]
                g_ref[slot_b, jr, cols] = (x * (1.0 + lax.erf(x * inv_sqrt2))).astype(BF16) * w[u][k]

    pieces = []
    for nc in range(tq // ncol):
        for mr in range(max(et, d) // mrow):
            if mr < et // mrow:
                pieces.append(functools.partial(piece_a, mr, nc))
            if mr < d // mrow:
                pieces.append(functools.partial(piece_c, mr, nc))
    blocks = [(il0, c) for il0 in range(0, ni, IL_GROUP) for c in range(tq // LANES)]
    lead = min(2, len(pieces))
    for k in range(lead):
        pieces[k]()
    per_piece = max(1, len(blocks) // max(1, len(pieces) - lead + 1))
    nxt = lead
    for k, (il, c) in enumerate(blocks):
        block_b(il, c)
        if (k + 1) % per_piece == 0 and nxt < len(pieces):
            pieces[nxt]()
            nxt += 1
    for k in range(nxt, len(pieces)):
        pieces[k]()


def _experts(h2t, dn, upt, e2, r2, e1, rk, tq, ni):
    d, t = h2t.shape
    n_exp = dn.shape[0] * 2
    nk = e1.shape[1]
    et = ni * nk
    ne = n_exp // et
    n_items = (t // tq) * ne
    a_item = lambda s: jnp.minimum(s, n_items - 1)
    b_item = lambda s: jnp.clip(s - 1, 0, n_items - 1)
    c_item = lambda s: jnp.clip(s - 2, 0, n_items - 1)
    per_j = pl.BlockSpec((PEER_HEADS, nk // 2, tq), lambda s: (0, 0, b_item(s) // ne))
    per_i = pl.BlockSpec((PEER_HEADS, ni, tq), lambda s: (0, b_item(s) % ne, b_item(s) // ne))
    return pl.pallas_call(
        functools.partial(_experts_kernel, nk=nk, ni=ni, tq=tq, ne=ne, n_items=n_items),
        grid=(n_items + 2,),
        in_specs=[pl.BlockSpec((d, tq), lambda s: (0, a_item(s) // ne)),
                  pl.BlockSpec((et // 2, d), lambda s: (a_item(s) % ne, 0)),
                  pl.BlockSpec((d // 2, et), lambda s: (0, c_item(s) % ne)),
                  per_j, per_j, per_i, per_i],
        out_specs=pl.BlockSpec((d, tq), lambda s: (0, c_item(s) // ne)),
        out_shape=jax.ShapeDtypeStruct((d, t), F32),
        scratch_shapes=[pltpu.VMEM((2, et, tq), F32), pltpu.VMEM((2, et, tq), BF16)],
        compiler_params=_cparams(("arbitrary",)),
        name="experts",
    )(h2t, dn, upt, e2, r2, e1, rk)


def _final_kernel(x1_ref, pt_ref, gt2_ref, g_ref, o_ref):
    x2 = x1_ref[...] + gt2_ref[0] * pt_ref[...].T
    ms = jnp.mean(x2 * x2, axis=-1, keepdims=True)
    o_ref[...] = x2 * lax.rsqrt(ms + EPS) * g_ref[...]


def _final(x1, peer_t, gt2, g, seq, tm):
    t, d = x1.shape
    per_b = seq // tm
    return pl.pallas_call(
        _final_kernel,
        grid=(t // tm,),
        in_specs=[pl.BlockSpec((tm, d), lambda i: (i, 0)),
                  pl.BlockSpec((d, tm), lambda i: (0, i)),
                  pl.BlockSpec((1, 1, d), lambda i: (i // per_b, 0, 0)),
                  pl.BlockSpec((1, d), lambda i: (0, 0))],
        out_specs=pl.BlockSpec((tm, d), lambda i: (i, 0)),
        out_shape=jax.ShapeDtypeStruct((t, d), F32),
        compiler_params=_cparams(("parallel",)),
        name="final",
    )(x1, peer_t, gt2, g)


def _t5_bucket(dist):
    dist = np.clip(dist, 0, None)
    max_exact = NUM_BUCKETS // 2
    large = max_exact + (np.log(np.maximum(dist, 1) / max_exact) / np.log(MAX_DISTANCE / max_exact)
                         * (NUM_BUCKETS - max_exact)).astype(np.int32)
    large = np.minimum(large, NUM_BUCKETS - 1)
    return np.where(dist < max_exact, dist, large).astype(np.int32)


def _pack_kernel(x_ref, o_ref, *, transpose):
    x = x_ref[...]
    if transpose:
        x = x.T
    o_ref[...] = pltpu.bitcast(x.astype(BF16), jnp.uint32)


def _pack_bf16_rows(x, transpose=False, tile=512):
    r, c = x.shape
    if transpose:
        in_spec = pl.BlockSpec((tile, c), lambda i: (i, 0))
        out_spec = pl.BlockSpec((c // 2, tile), lambda i: (0, i))
        out_shape = jax.ShapeDtypeStruct((c // 2, r), jnp.uint32)
    else:
        in_spec = pl.BlockSpec((tile, c), lambda i: (i, 0))
        out_spec = pl.BlockSpec((tile // 2, c), lambda i: (i, 0))
        out_shape = jax.ShapeDtypeStruct((r // 2, c), jnp.uint32)
    return pl.pallas_call(
        functools.partial(_pack_kernel, transpose=transpose),
        grid=(r // tile,),
        in_specs=[in_spec],
        out_specs=out_spec,
        out_shape=out_shape,
        compiler_params=_cparams(("parallel",)),
        name="pack_t" if transpose else "pack",
    )(x)


def _row_tile(seq, want):
    tm = min(want, seq)
    assert seq % tm == 0
    return tm


def _fsplit_placement():
    place = np.zeros((N_FSPLIT, LANES, FOX_HEADS * LANES), np.float32)
    for p in range(N_FSPLIT):
        for h in range(FOX_HEADS):
            place[p, h, h * LANES + HEAD_DIM + p] = 1.0
    return jnp.asarray(place, BF16)


def kernel(x, c, norm_mix_g, norm_ffn_g, w_ada, b_ada, w_in, b_forget, sinks, w_branch_a,
           w_branch_b, w_out, rel_bias, w_query, sub_keys, expert_down, expert_up, final_norm_g):
    bsz, seq, d = x.shape
    depth = w_ada.shape[0]
    t = bsz * seq
    tm = _row_tile(seq, 512)
    assert seq % FOX_TK == 0 and tm % FOX_TK == 0

    dist = np.arange(BLOCK)[:, None] + BLOCK - np.arange(2 * BLOCK)[None, :]
    bucket = jnp.asarray(_t5_bucket(dist), jnp.int32)
    place = _fsplit_placement()

    o_a = SWA_WIDTH + 2 * SWA_KV_WIDTH
    o_q, o_k, o_v = o_a, o_a + FOX_WIDTH, o_a + 2 * FOX_WIDTH
    o_b = o_a + 3 * FOX_WIDTH
    o_f = o_b + FOX_HEADS
    pad_head = LANES - HEAD_DIM

    xc = x.reshape(t, d)
    for l in range(depth):
        mod = _ada(c, w_ada[l], b_ada[l])
        sh1, sc1, gt1, sh2, sc2, gt2 = [m.reshape(bsz, 1, d) for m in jnp.split(mod, N_MOD, axis=-1)]

        w = w_in[l]
        wa = w[:, :o_a].astype(BF16)
        wq3 = w[:, o_q:o_k].T.reshape(FOX_HEADS, HEAD_DIM, d)
        wqt = jnp.pad(wq3, ((0, 0), (0, pad_head), (0, 0))).reshape(FOX_HEADS * LANES, d).astype(BF16)
        wk3 = w[:, o_k:o_v].reshape(d, FOX_HEADS, HEAD_DIM)
        wk = jnp.pad(wk3, ((0, 0), (0, 0), (0, pad_head))).reshape(d, FOX_HEADS * LANES).astype(BF16)
        wvt = w[:, o_v:o_b].T.astype(BF16)
        wf = jnp.pad(w[:, o_b:o_f], ((0, 0), (0, LANES - FOX_HEADS)))
        wg = w[:, o_f:].astype(BF16)
        qa, ka, va, qt, kpad, vt, fl, ga, gb = _inproj(
            xc, norm_mix_g[l].reshape(1, d), sc1, sh1, wa, wqt, wk, wvt, wf, wg, seq, tm)

        bpad = jnp.pad(b_forget[l], (0, LANES - FOX_HEADS)).reshape(1, LANES)
        kf = _kf(fl.reshape(bsz, seq, LANES), bpad, kpad, place)

        oa = _swa(qa, ka, va, sinks[l], rel_bias.astype(F32), bucket, bsz, seq)
        ob = _fox(qt, kf, vt, bsz, seq, _row_tile(seq, 512))

        x1, h2t, q = _mix(xc, oa, ob, ga, gb, gt1, norm_ffn_g[l].reshape(1, d), sc2, sh2,
                          w_branch_a[l].astype(BF16), w_branch_b[l].astype(BF16),
                          w_out[l].astype(BF16), w_query[l].astype(BF16), seq, tm)

        e2, r2, e1, rk = _route(q, sub_keys[l], _row_tile(seq, 256))
        ni = 8
        peer_t = _experts(h2t, _pack_bf16_rows(expert_down[l]), _pack_bf16_rows(expert_up[l], transpose=True),
                          e2, r2, e1, rk, _row_tile(seq, EXPERT_TQ), ni)

        g_next = final_norm_g.reshape(1, d)
        xc = _final(x1, peer_t, gt2, g_next, seq, tm)
        assert depth == 1
    return xc.reshape(bsz, seq, d)
```

```python
import functools
import math

import numpy as np
import jax
import jax.numpy as jnp
from jax import lax
from jax.experimental import pallas as pl
from jax.experimental.pallas import tpu as pltpu

HEAD_DIM = 64
SWA_Q_HEADS = 8
SWA_KV_HEADS = 2
SWA_GROUP = SWA_Q_HEADS // SWA_KV_HEADS
WINDOW = 128
BLOCK = 128
FOX_HEADS = 8
SWA_WIDTH = SWA_Q_HEADS * HEAD_DIM
SWA_KV_WIDTH = SWA_KV_HEADS * HEAD_DIM
FOX_WIDTH = FOX_HEADS * HEAD_DIM
NUM_BUCKETS = 32
MAX_DISTANCE = 128
PEER_HEADS = 8
PEER_TOPK = 16
PEER_HALF = 64
N_MOD = 6
EPS = 1e-6
NEG = -1e30
LANES = 128
BF16_SUBLANES = 16
VMEM_LIMIT = 56 * 1024 * 1024
LOG2E = math.log2(math.e)
N_FSPLIT = 3
FOX_TK = 256
SWA_STEP_BLOCKS = 2
MXU_PIECE_ROWS = 512
MXU_PIECE_COLS = 256
EXPERT_TQ = 1024
IL_GROUP = 1

F32 = jnp.float32
BF16 = jnp.bfloat16


def _cparams(sem):
    return pltpu.CompilerParams(dimension_semantics=sem, vmem_limit_bytes=VMEM_LIMIT)


def _dot(a, b):
    return jnp.dot(a, b, preferred_element_type=F32)


def _dot_nt(a, b):
    return lax.dot_general(a, b, (((1,), (1,)), ((), ())), preferred_element_type=F32)


def _rms_mod(x, g, sc, sh):
    ms = jnp.mean(x * x, axis=-1, keepdims=True)
    return (x * lax.rsqrt(ms + EPS) * g) * (1.0 + sc) + sh


def _ada_kernel(c_ref, w_ref, b_ref, o_ref):
    c = c_ref[...]
    ca = c * jax.nn.sigmoid(c)
    o_ref[...] = _dot(ca, w_ref[...]) + b_ref[...]


def _ada(c, w_ada, b_ada):
    bsz, d = c.shape
    n = w_ada.shape[1]
    tn = d
    return pl.pallas_call(
        _ada_kernel,
        grid=(n // tn,),
        in_specs=[pl.BlockSpec((bsz, d), lambda j: (0, 0)),
                  pl.BlockSpec((d, tn), lambda j: (0, j)),
                  pl.BlockSpec((1, tn), lambda j: (0, j))],
        out_specs=pl.BlockSpec((bsz, tn), lambda j: (0, j)),
        out_shape=jax.ShapeDtypeStruct((bsz, n), F32),
        compiler_params=_cparams(("arbitrary",)),
        name="ada",
    )(c, w_ada, b_ada.reshape(1, n))


def _inproj_kernel(x_ref, g_ref, sc_ref, sh_ref, wqat_ref, wka_ref, wvat_ref, wqt_ref, wk_ref, wvt_ref,
                   wf_ref, wg_ref,
                   qa_ref, ka_ref, va_ref, qt_ref, kp_ref, vt_ref, fl_ref, ga_ref, gb_ref):
    h = _rms_mod(x_ref[...], g_ref[...], sc_ref[0], sh_ref[0])
    hb = h.astype(BF16)
    ht = h.T.astype(BF16)
    scale = HEAD_DIM ** -0.5
    qa_ref[...] = (_dot(wqat_ref[...], ht) * scale).astype(BF16).reshape(qa_ref.shape)
    ka_ref[...] = _dot(hb, wka_ref[...]).astype(BF16)
    va_ref[...] = _dot(wvat_ref[...], ht).astype(BF16)
    qt = _dot(wqt_ref[...], ht) * (scale * LOG2E)
    r = lax.broadcasted_iota(jnp.int32, qt.shape, 0) % LANES
    qt = jnp.where((r >= HEAD_DIM) & (r < HEAD_DIM + N_FSPLIT), 1.0, qt)
    qt_ref[...] = qt.astype(BF16).reshape(qt_ref.shape)
    kp_ref[...] = _dot(hb, wk_ref[...]).astype(BF16)
    vt = _dot(wvt_ref[...], ht).astype(BF16)
    tk = vt_ref.shape[2]
    for cb in range(vt_ref.shape[0]):
        vt_ref[cb] = vt[:, cb * tk:(cb + 1) * tk]
    fl_ref[...] = _dot(h, wf_ref[...])
    d = ga_ref.shape[1]
    pg = _dot(hb, wg_ref[...])
    ga_ref[...] = jax.nn.sigmoid(pg[:, :d]).astype(BF16)
    gb_ref[...] = jax.nn.sigmoid(pg[:, d:]).astype(BF16)


def _inproj(x2d, g, sc1, sh1, wqat, wka, wvat, wqt, wk, wvt, wf, wg, seq, tm):
    t, d = x2d.shape
    per_b = seq // tm
    hp = FOX_HEADS * LANES
    row = lambda w: pl.BlockSpec((tm, w), lambda i: (i, 0))
    col = lambda r: pl.BlockSpec((r, tm), lambda i: (0, i))
    full = lambda a: pl.BlockSpec(a.shape, lambda i: (0,) * a.ndim)
    mod = pl.BlockSpec((1, 1, d), lambda i: (i // per_b, 0, 0))
    out_shape = [jax.ShapeDtypeStruct((SWA_Q_HEADS, LANES, t), BF16),
                 jax.ShapeDtypeStruct((t, SWA_KV_HEADS * LANES), BF16),
                 jax.ShapeDtypeStruct((SWA_KV_WIDTH, t), BF16),
                 jax.ShapeDtypeStruct((FOX_HEADS, LANES, t), BF16),
                 jax.ShapeDtypeStruct((t, hp), BF16),
                 jax.ShapeDtypeStruct((t // FOX_TK, FOX_WIDTH, FOX_TK), BF16),
                 jax.ShapeDtypeStruct((t, LANES), F32),
                 jax.ShapeDtypeStruct((t, d), BF16), jax.ShapeDtypeStruct((t, d), BF16)]
    out_specs = [pl.BlockSpec((SWA_Q_HEADS, LANES, tm), lambda i: (0, 0, i)),
                 row(SWA_KV_HEADS * LANES), col(SWA_KV_WIDTH),
                 pl.BlockSpec((FOX_HEADS, LANES, tm), lambda i: (0, 0, i)),
                 row(hp),
                 pl.BlockSpec((tm // FOX_TK, FOX_WIDTH, FOX_TK), lambda i: (i, 0, 0)),
                 row(LANES), row(d), row(d)]
    return pl.pallas_call(
        _inproj_kernel,
        grid=(t // tm,),
        in_specs=[row(d), full(g), mod, mod, full(wqat), full(wka), full(wvat), full(wqt), full(wk), full(wvt),
                  full(wf), full(wg)],
        out_specs=out_specs,
        out_shape=out_shape,
        compiler_params=_cparams(("parallel",)),
        name="inproj",
    )(x2d, g, sc1, sh1, wqat, wka, wvat, wqt, wk, wvt, wf, wg)


def _kf_kernel(fl_ref, b_ref, kp_ref, place_ref, ko_ref):
    z = fl_ref[0] + b_ref[...]
    lf = jnp.minimum(z, 0.0) * LOG2E - jnp.log2(1.0 + jnp.exp(-jnp.abs(z)))
    s = lf.shape[0]
    rows = lax.broadcasted_iota(jnp.int32, lf.shape, 0)
    acc = lf
    k = 1
    while k < s:
        shifted = pltpu.roll(acc, k, 0)
        acc = acc + jnp.where(rows >= k, shifted, 0.0)
        k *= 2
    rem = -acc
    placed = None
    for piece in range(N_FSPLIT):
        part = rem.astype(BF16)
        rem = rem - part.astype(F32)
        y = _dot(part, place_ref[piece])
        placed = y if placed is None else placed + y
    ko_ref[...] = (kp_ref[...].astype(F32) + placed).astype(BF16)


def _kf(fl3, bpad, kpad, place):
    bsz, s, _ = fl3.shape
    hp = kpad.shape[1]
    return pl.pallas_call(
        _kf_kernel,
        grid=(bsz,),
        in_specs=[pl.BlockSpec((1, s, LANES), lambda b: (b, 0, 0)),
                  pl.BlockSpec((1, LANES), lambda b: (0, 0)),
                  pl.BlockSpec((s, hp), lambda b: (b, 0)),
                  pl.BlockSpec(place.shape, lambda b: (0, 0, 0))],
        out_specs=pl.BlockSpec((s, hp), lambda b: (b, 0)),
        out_shape=jax.ShapeDtypeStruct(kpad.shape, BF16),
        compiler_params=_cparams(("parallel",)),
        name="kf",
    )(fl3, bpad, kpad, place)


def _swa_kernel(sink_ref, rel_ref, q_ref, kp_ref, kc_ref, vp_ref, vc_ref, bucket_ref, o_ref,
                bias_ref, sinkrow_ref):
    n = pl.program_id(1)

    @pl.when((pl.program_id(0) == 0) & (n == 0))
    def _():
        bucket = bucket_ref[...]
        for h in range(SWA_Q_HEADS):
            tile = jnp.full(bucket.shape, NEG, F32)
            for bk in range(NUM_BUCKETS):
                tile = jnp.where(bucket == bk, rel_ref[bk, h], tile)
            g, hl = divmod(h, SWA_GROUP)
            bias_ref[g, :, hl * BLOCK:(hl + 1) * BLOCK] = tile
            sinkrow_ref[g, :, hl * BLOCK:(hl + 1) * BLOCK] = jnp.full((8, BLOCK), sink_ref[h], F32)

    def block(first):
        assert SWA_STEP_BLOCKS == 2
        kcur = kc_ref[...]
        kbands =[jnp.concatenate([kp_ref[...], kcur[:BLOCK]], axis=0), kcur]
        for sb in range(SWA_STEP_BLOCKS):
            outs = []
            for g in range(SWA_KV_HEADS):
                qg = jnp.concatenate([q_ref[g * SWA_GROUP + hl, :, sb * BLOCK:(sb + 1) * BLOCK]
                                      for hl in range(SWA_GROUP)], axis=1)
                s = _dot(kbands[sb][:, g * LANES:(g + 1) * LANES], qg) + bias_ref[g]
                if first and sb == 0:
                    rows = lax.broadcasted_iota(jnp.int32, s.shape, 0)
                    s = jnp.where(rows >= BLOCK, s, NEG)
                sink = sinkrow_ref[g, 0:1, :]
                m = jnp.maximum(jnp.max(s, axis=0, keepdims=True), sink)
                p = jnp.exp(s - m)
                denom = jnp.sum(p, axis=0, keepdims=True) + jnp.exp(sink - m)
                rows_v = slice(g * HEAD_DIM, (g + 1) * HEAD_DIM)
                if sb == 0:
                    vg = jnp.concatenate([vp_ref[rows_v, :], vc_ref[rows_v, :BLOCK]], axis=1)
                else:
                    vg = vc_ref[rows_v, :]
                o = _dot(vg, p.astype(BF16)) / denom
                outs += [o[:, hl * BLOCK:(hl + 1) * BLOCK] for hl in range(SWA_GROUP)]
            o_ref[sb * BLOCK:(sb + 1) * BLOCK, :] = jnp.concatenate(outs, axis=0).T.astype(o_ref.dtype)

    @pl.when(n == 0)
    def _():
        block(True)

    @pl.when(n > 0)
    def _():
        block(False)


def _swa(qat, kap, vat, sinks, rel_bias, bucket_t, bsz, seq):
    sbk = SWA_STEP_BLOCKS
    assert seq % (sbk * BLOCK) == 0
    nb = seq // BLOCK
    ns = nb // sbk
    t = bsz * seq
    step = sbk * BLOCK
    cur_r = lambda b, n: (b * ns + n, 0)
    prev_r = lambda b, n: (b * nb + jnp.maximum(sbk * n - 1, 0), 0)
    cur_c = lambda b, n: (0, b * ns + n)
    prev_c = lambda b, n: (0, b * nb + jnp.maximum(sbk * n - 1, 0))
    kw = SWA_KV_HEADS * LANES
    return pl.pallas_call(
        _swa_kernel,
        grid=(bsz, ns),
        in_specs=[pl.BlockSpec(memory_space=pltpu.SMEM),
                  pl.BlockSpec(memory_space=pltpu.SMEM),
                  pl.BlockSpec((SWA_Q_HEADS, LANES, step), lambda b, n: (0, 0, b * ns + n)),
                  pl.BlockSpec((BLOCK, kw), prev_r),
                  pl.BlockSpec((step, kw), cur_r),
                  pl.BlockSpec((SWA_KV_WIDTH, BLOCK), prev_c),
                  pl.BlockSpec((SWA_KV_WIDTH, step), cur_c),
                  pl.BlockSpec(bucket_t.shape, lambda b, n: (0, 0))],
        out_specs=pl.BlockSpec((step, SWA_WIDTH), cur_r),
        out_shape=jax.ShapeDtypeStruct((t, SWA_WIDTH), BF16),
        scratch_shapes=[pltpu.VMEM((SWA_KV_HEADS, 2 * BLOCK, SWA_GROUP * BLOCK), F32),
                        pltpu.VMEM((SWA_KV_HEADS, 8, SWA_GROUP * BLOCK), F32)],
        compiler_params=_cparams(("arbitrary", "arbitrary")),
        name="swa",
    )(sinks, rel_bias, qat, kap, kap, vat, vat, bucket_t)


def _fox_kernel(q_ref, k_ref, v_ref, o_ref, m_ref, l_ref, acc_ref, *, tq, tk):
    qi = pl.program_id(1)
    m_ref[...] = jnp.full(m_ref.shape, NEG, F32)
    l_ref[...] = jnp.zeros(l_ref.shape, F32)
    acc_ref[...] = jnp.zeros(acc_ref.shape, F32)
    per_q = tq // tk

    def block(j, masked):
        krow0 = pl.multiple_of(j * tk, tk)
        if masked:
            kpos = j * tk + lax.broadcasted_iota(jnp.int32, (tk, tq), 0)
            qpos = qi * tq + lax.broadcasted_iota(jnp.int32, (tk, tq), 1)
            causal = kpos <= qpos

        def scores(h):
            kblk = k_ref[pl.ds(krow0, tk), h * LANES:(h + 1) * LANES]
            return _dot(kblk, q_ref[h])

        s_next = scores(0)
        for h in range(FOX_HEADS):
            s = s_next
            if h + 1 < FOX_HEADS:
                s_next = scores(h + 1)
            if masked:
                s = jnp.where(causal, s, NEG)
            m_old = m_ref[h:h + 1, :]
            m_new = jnp.maximum(m_old, jnp.max(s, axis=0, keepdims=True))
            alpha = jnp.exp2(m_old - m_new)
            p = jnp.exp2(s - m_new)
            l_ref[h:h + 1, :] = alpha * l_ref[h:h + 1, :] + jnp.sum(p, axis=0, keepdims=True)
            m_ref[h:h + 1, :] = m_new
            rows = slice(h * HEAD_DIM, (h + 1) * HEAD_DIM)
            pv = _dot(v_ref[j, rows, :], p.astype(BF16))
            acc_ref[rows, :] = alpha * acc_ref[rows, :] + pv

    def body(j, carry):
        block(j, False)
        return carry

    lax.fori_loop(0, qi * per_q, body, 0)
    for jj in range(per_q):
        block(qi * per_q + jj, True)

    outs = []
    for h in range(FOX_HEADS):
        rows = slice(h * HEAD_DIM, (h + 1) * HEAD_DIM)
        outs.append(acc_ref[rows, :] / l_ref[h:h + 1, :])
    o_ref[...] = jnp.concatenate(outs, axis=0).T.astype(o_ref.dtype)


def _fox(qt, kf, vt, bsz, seq, tq):
    tk = FOX_TK
    nq = seq // tq
    hp = kf.shape[1]
    return pl.pallas_call(
        functools.partial(_fox_kernel, tq=tq, tk=tk),
        grid=(bsz, nq),
        in_specs=[pl.BlockSpec((FOX_HEADS, LANES, tq), lambda b, i: (0, 0, b * nq + i)),
                  pl.BlockSpec((seq, hp), lambda b, i: (b, 0)),
                  pl.BlockSpec((seq // tk, FOX_WIDTH, tk), lambda b, i: (b, 0, 0))],
        out_specs=pl.BlockSpec((tq, FOX_WIDTH), lambda b, i: (b * nq + i, 0)),
        out_shape=jax.ShapeDtypeStruct((bsz * seq, FOX_WIDTH), BF16),
        scratch_shapes=[pltpu.VMEM((FOX_HEADS, tq), F32),
                        pltpu.VMEM((FOX_HEADS, tq), F32),
                        pltpu.VMEM((FOX_WIDTH, tq), F32)],
        compiler_params=_cparams(("parallel", "arbitrary")),
        name="fox",
    )(qt, kf, vt)


def _mix_kernel(x_ref, oa_ref, ob_ref, ga_ref, gb_ref, gt1_ref, g2_ref, sc2_ref, sh2_ref,
                wa_ref, wb_ref, wo_ref, wq_ref, x1_ref, h2t_ref, q_ref):
    ya = _dot(oa_ref[...], wa_ref[...])
    yb = _dot(ob_ref[...], wb_ref[...])
    mixed = ga_ref[...].astype(F32) * ya + gb_ref[...].astype(F32) * yb
    x1 = x_ref[...] + gt1_ref[0] * _dot(mixed.astype(BF16), wo_ref[...])
    x1_ref[...] = x1
    h2 = _rms_mod(x1, g2_ref[...], sc2_ref[0], sh2_ref[0])
    h2t_ref[...] = h2.T.astype(BF16)
    q_ref[...] = _dot(h2.astype(BF16), wq_ref[...])


def _mix(x2d, oa, ob, ga, gb, gt1, g2, sc2, sh2, wa, wb, wo, wq, seq, tm):
    t, d = x2d.shape
    per_b = seq // tm
    row = lambda w: pl.BlockSpec((tm, w), lambda i: (i, 0))
    full = lambda a: pl.BlockSpec(a.shape, lambda i: (0,) * a.ndim)
    mod = pl.BlockSpec((1, 1, d), lambda i: (i // per_b, 0, 0))
    return pl.pallas_call(
        _mix_kernel,
        grid=(t // tm,),
        in_specs=[row(d), row(SWA_WIDTH), row(FOX_WIDTH), row(d), row(d), mod, full(g2), mod, mod,
                  full(wa), full(wb), full(wo), full(wq)],
        out_specs=[row(d), pl.BlockSpec((d, tm), lambda i: (0, i)), row(wq.shape[1])],
        out_shape=[jax.ShapeDtypeStruct((t, d), F32), jax.ShapeDtypeStruct((d, t), BF16),
                   jax.ShapeDtypeStruct((t, wq.shape[1]), F32)],
        compiler_params=_cparams(("parallel",)),
        name="mix",
    )(x2d, oa, ob, ga, gb, gt1, g2, sc2, sh2, wa, wb, wo, wq)


def _top_desc(vals, count):
    rowidx = lax.broadcasted_iota(jnp.int32, vals.shape, 0).astype(F32)
    tops = []
    work = vals
    for _ in range(count):
        mx = jnp.max(work, axis=0, keepdims=True)
        first = jnp.min(jnp.where(work == mx, rowidx, float(vals.shape[0])), axis=0, keepdims=True)
        tops.append(mx)
        work = jnp.where(rowidx == first, -jnp.inf, work)
    return tops, work


def _sort_network(n):
    def merge(lo, hi, r):
        step = r * 2
        if step < hi - lo:
            yield from merge(lo, hi, step)
            yield from merge(lo + r, hi, step)
            yield from [(i, i + r) for i in range(lo + r, hi - r, step)]
        else:
            yield (lo, lo + r)

    def sort(lo, hi):
        if hi - lo >= 1:
            mid = lo + (hi - lo) // 2
            yield from sort(lo, mid)
            yield from sort(mid + 1, hi)
            yield from merge(lo, hi, 1)

    return list(sort(0, n - 1))


def _top_desc_sorted(vals, count):
    n = vals.shape[0]
    depth = n // 8
    lists = [vals[8 * r:8 * (r + 1), :] for r in range(depth)]
    for a, b in _sort_network(depth):
        hi, lo = jnp.maximum(lists[a], lists[b]), jnp.minimum(lists[a], lists[b])
        lists[a], lists[b] = hi, lo
    tops = []
    for k in range(count):
        mx = jnp.max(lists[0], axis=0, keepdims=True)
        tops.append(mx)
        need = min(depth, count - 1 - k)
        if need == 0:
            break
        pop = lists[0] == mx
        for r in range(need):
            nxt = lists[r + 1] if r + 1 < depth else -jnp.inf
            lists[r] = jnp.where(pop, nxt, lists[r])
    return tops


def _route_kernel(q_ref, sk_ref, e2_ref, r2_ref, e1_ref, rk_ref):
    q = q_ref[...]
    nk = sk_ref.shape[2]
    sorted_ok = nk % 8 == 0 and (nk // 8) & (nk // 8 - 1) == 0
    top = _top_desc_sorted if sorted_ok else (lambda v, n: _top_desc(v, n)[0])
    k = PEER_TOPK
    for h in range(PEER_HEADS):
        q1 = q[:, (2 * h) * PEER_HALF:(2 * h + 1) * PEER_HALF]
        q2 = q[:, (2 * h + 1) * PEER_HALF:(2 * h + 2) * PEER_HALF]
        s1 = _dot_nt(sk_ref[h, 0], q1)
        s2 = _dot_nt(sk_ref[h, 1], q2)
        a = top(s1, k)
        b = top(s2, k)
        c, n1 = _top_pair_sums(a, b, k)
        z = jnp.zeros_like(c[0])
        for r in range(k):
            z = z + jnp.exp(c[r] - c[0])
        rk = jnp.zeros(s2.shape, F32)
        cnt = jnp.zeros(s1.shape, F32)
        for r in range(k):
            rk = jnp.where(b[r] > s2, float(r + 1), rk)
        for r in reversed(range(k)):
            cnt = jnp.where(s1 == a[r], n1[r], cnt)
        e2_ref[h] = pltpu.bitcast(jnp.exp(s2 - b[0]).astype(BF16), jnp.uint32)
        r2_ref[h] = pltpu.bitcast(rk.astype(BF16), jnp.uint32)
        e1_ref[h] = jnp.exp(s1 - a[0]) * (0.5 / z)
        rk_ref[h] = cnt


def _top_pair_sums(a, b, k):
    assert k == 2 * 8, "two sublane groups of lists"
    t = a[0].shape[1]
    a_lo = jnp.concatenate(a[:8], axis=0)
    a_hi = jnp.concatenate(a[8:], axis=0)
    sub = lax.broadcasted_iota(jnp.int32, (8, t), 0)
    idx_lo = sub.astype(F32)
    idx_hi = idx_lo + 8.0
    neg = jnp.full((8, t), -jnp.inf, F32)
    depth = [a_lo + b[0]] + [jnp.where(sub < k // (d + 1), a_lo + b[d], neg) for d in range(1, k)]
    head_hi = a_hi + b[0]
    n_lo = jnp.zeros((8, t), F32)
    n_hi = jnp.zeros((8, t), F32)
    sums = []
    for step in range(k):
        mx = jnp.max(jnp.maximum(depth[0], head_hi), axis=0, keepdims=True)
        sums.append(mx)
        first = jnp.min(jnp.minimum(jnp.where(depth[0] == mx, idx_lo, float(k)),
                                    jnp.where(head_hi == mx, idx_hi, float(k))), axis=0, keepdims=True)
        pop_lo = idx_lo == first
        pop_hi = idx_hi == first
        n_lo = n_lo + jnp.where(pop_lo, 1.0, 0.0)
        n_hi = n_hi + jnp.where(pop_hi, 1.0, 0.0)
        left = k - 1 - step
        for d in range(min(left, k - 1)):
            nxt = depth[d + 1] if d + 1 < k else neg
            depth[d] = jnp.where(pop_lo, nxt, depth[d])
        head_hi = jnp.where(pop_hi, neg, head_hi)
    n1 = [n_lo[r:r + 1, :] for r in range(8)] + [n_hi[r:r + 1, :] for r in range(8)]
    return sums, n1


def _route(q, sub_keys, tr):
    t, w = q.shape
    nk = sub_keys.shape[2]
    spec = pl.BlockSpec((PEER_HEADS, nk, tr), lambda i: (0, 0, i))
    shp = jax.ShapeDtypeStruct((PEER_HEADS, nk, t), F32)
    spec_j = pl.BlockSpec((PEER_HEADS, nk // 2, tr), lambda i: (0, 0, i))
    shp_j = jax.ShapeDtypeStruct((PEER_HEADS, nk // 2, t), jnp.uint32)
    return pl.pallas_call(
        _route_kernel,
        grid=(t // tr,),
        in_specs=[pl.BlockSpec((tr, w), lambda i: (i, 0)),
                  pl.BlockSpec(sub_keys.shape, lambda i: (0, 0, 0, 0))],
        out_specs=[spec_j, spec_j, spec, spec],
        out_shape=[shp_j, shp_j, shp, shp],
        compiler_params=_cparams(("parallel",)),
        name="route",
    )(q, sub_keys)


def _experts_kernel(h2t_ref, dn_ref, upt_ref, e2_ref, r2_ref, e1_ref, rk_ref, o_ref, act_ref, g_ref,
                    *, nk, ni, tq, ne, n_items):
    s = pl.program_id(0)
    c_item = jnp.clip(s - 2, 0, n_items - 1)

    @pl.when(s == 0)
    def _():
        act_ref[...] = jnp.zeros(act_ref.shape, F32)
        g_ref[...] = jnp.zeros(g_ref.shape, BF16)

    @pl.when(c_item % ne == 0)
    def _():
        o_ref[...] = jnp.zeros(o_ref.shape, F32)

    slot_a = s % 2
    slot_b = (s + 1) % 2
    et = ni * nk
    d = o_ref.shape[0]
    inv_sqrt2 = 2.0 ** -0.5
    mrow = min(MXU_PIECE_ROWS, et, d)
    ncol = min(MXU_PIECE_COLS, tq)

    def piece_a(mr, nc):
        rows, cols = slice(mr * mrow, (mr + 1) * mrow), slice(nc * ncol, (nc + 1) * ncol)
        words = slice(mr * mrow // 2, (mr + 1) * mrow // 2)
        act_ref[slot_a, rows, cols] = _dot(pltpu.bitcast(dn_ref[words, :], BF16), h2t_ref[:, cols])

    def piece_c(mr, nc):
        rows, cols = slice(mr * mrow, (mr + 1) * mrow), slice(nc * ncol, (nc + 1) * ncol)
        words = slice(mr * mrow // 2, (mr + 1) * mrow // 2)
        o_ref[rows, cols] += _dot(pltpu.bitcast(upt_ref[words, :], BF16), g_ref[slot_a, :, cols])

    def block_b(il0, c):
        cols = slice(c * LANES, (c + 1) * LANES)
        sub = BF16_SUBLANES
        ils = range(il0, il0 + IL_GROUP)
        w = [[None] * (nk // sub) for _ in ils]
        for h in range(PEER_HEADS):
            e1b = [jnp.broadcast_to(e1_ref[h, il:il + 1, cols], (sub, LANES)).astype(BF16) for il in ils]
            rkb = [jnp.broadcast_to(rk_ref[h, il:il + 1, cols], (sub, LANES)).astype(BF16) for il in ils]
            for k in range(nk // sub):
                wr = slice(k * sub // 2, (k + 1) * sub // 2)
                e2v = pltpu.bitcast(e2_ref[h, wr, cols], BF16)
                r2v = pltpu.bitcast(r2_ref[h, wr, cols], BF16)
                for u in range(IL_GROUP):
                    p = e2v * e1b[u]
                    sel = jnp.where(r2v < rkb[u], p, jnp.zeros_like(p))
                    w[u][k] = sel if w[u][k] is None else w[u][k] + sel
        for u, il in enumerate(ils):
            for k in range(nk // sub):
                jr = slice(il * nk + k * sub, il * nk + (k + 1) * sub)
                x = act_ref[slot_b, jr, cols]
                g_ref[slot_b, jr, cols] = (x * (1.0 + lax.erf(x * inv_sqrt2))).astype(BF16) * w[u][k]

    pieces = []
    for nc in range(tq // ncol):
        for mr in range(max(et, d) // mrow):
            if mr < et // mrow:
                pieces.append(functools.partial(piece_a, mr, nc))
            if mr < d // mrow:
                pieces.append(functools.partial(piece_c, mr, nc))
    blocks = [(il0, c) for il0 in range(0, ni, IL_GROUP) for c in range(tq // LANES)]
    lead = min(2, len(pieces))
    for k in range(lead):
        pieces[k]()
    per_piece = max(1, len(blocks) // max(1, len(pieces) - lead + 1))
    nxt = lead
    for k, (il, c) in enumerate(blocks):
        block_b(il, c)
        if (k + 1) % per_piece == 0 and nxt < len(pieces):
            pieces[nxt]()
            nxt += 1
    for k in range(nxt, len(pieces)):
        pieces[k]()


def _experts(h2t, dn, upt, e2, r2, e1, rk, tq, ni):
    d, t = h2t.shape
    n_exp = dn.shape[0] * 2
    nk = e1.shape[1]
    et = ni * nk
    ne = n_exp // et
    n_items = (t // tq) * ne
    a_item = lambda s: jnp.minimum(s, n_items - 1)
    b_item = lambda s: jnp.clip(s - 1, 0, n_items - 1)
    c_item = lambda s: jnp.clip(s - 2, 0, n_items - 1)
    per_j = pl.BlockSpec((PEER_HEADS, nk // 2, tq), lambda s: (0, 0, b_item(s) // ne))
    per_i = pl.BlockSpec((PEER_HEADS, ni, tq), lambda s: (0, b_item(s) % ne, b_item(s) // ne))
    return pl.pallas_call(
        functools.partial(_experts_kernel, nk=nk, ni=ni, tq=tq, ne=ne, n_items=n_items),
        grid=(n_items + 2,),
        in_specs=[pl.BlockSpec((d, tq), lambda s: (0, a_item(s) // ne)),
                  pl.BlockSpec((et // 2, d), lambda s: (a_item(s) % ne, 0)),
                  pl.BlockSpec((d // 2, et), lambda s: (0, c_item(s) % ne)),
                  per_j, per_j, per_i, per_i],
        out_specs=pl.BlockSpec((d, tq), lambda s: (0, c_item(s) // ne)),
        out_shape=jax.ShapeDtypeStruct((d, t), F32),
        scratch_shapes=[pltpu.VMEM((2, et, tq), F32), pltpu.VMEM((2, et, tq), BF16)],
        compiler_params=_cparams(("arbitrary",)),
        name="experts",
    )(h2t, dn, upt, e2, r2, e1, rk)


def _final_kernel(x1_ref, pt_ref, gt2_ref, g_ref, o_ref):
    x2 = x1_ref[...] + gt2_ref[0] * pt_ref[...].T
    ms = jnp.mean(x2 * x2, axis=-1, keepdims=True)
    o_ref[...] = x2 * lax.rsqrt(ms + EPS) * g_ref[...]


def _final(x1, peer_t, gt2, g, seq, tm):
    t, d = x1.shape
    per_b = seq // tm
    return pl.pallas_call(
        _final_kernel,
        grid=(t // tm,),
        in_specs=[pl.BlockSpec((tm, d), lambda i: (i, 0)),
                  pl.BlockSpec((d, tm), lambda i: (0, i)),
                  pl.BlockSpec((1, 1, d), lambda i: (i // per_b, 0, 0)),
                  pl.BlockSpec((1, d), lambda i: (0, 0))],
        out_specs=pl.BlockSpec((tm, d), lambda i: (i, 0)),
        out_shape=jax.ShapeDtypeStruct((t, d), F32),
        compiler_params=_cparams(("parallel",)),
        name="final",
    )(x1, peer_t, gt2, g)


def _t5_bucket(dist):
    dist = np.clip(dist, 0, None)
    max_exact = NUM_BUCKETS // 2
    large = max_exact + (np.log(np.maximum(dist, 1) / max_exact) / np.log(MAX_DISTANCE / max_exact)
                         * (NUM_BUCKETS - max_exact)).astype(np.int32)
    large = np.minimum(large, NUM_BUCKETS - 1)
    return np.where(dist < max_exact, dist, large).astype(np.int32)


def _pack_kernel(x_ref, o_ref, *, transpose):
    x = x_ref[...]
    if transpose:
        x = x.T
    o_ref[...] = pltpu.bitcast(x.astype(BF16), jnp.uint32)


def _pack_bf16_rows(x, transpose=False, tile=512):
    r, c = x.shape
    if transpose:
        in_spec = pl.BlockSpec((tile, c), lambda i: (i, 0))
        out_spec = pl.BlockSpec((c // 2, tile), lambda i: (0, i))
        out_shape = jax.ShapeDtypeStruct((c // 2, r), jnp.uint32)
    else:
        in_spec = pl.BlockSpec((tile, c), lambda i: (i, 0))
        out_spec = pl.BlockSpec((tile // 2, c), lambda i: (i, 0))
        out_shape = jax.ShapeDtypeStruct((r // 2, c), jnp.uint32)
    return pl.pallas_call(
        functools.partial(_pack_kernel, transpose=transpose),
        grid=(r // tile,),
        in_specs=[in_spec],
        out_specs=out_spec,
        out_shape=out_shape,
        compiler_params=_cparams(("parallel",)),
        name="pack_t" if transpose else "pack",
    )(x)


def _row_tile(seq, want):
    tm = min(want, seq)
    assert seq % tm == 0
    return tm


def _fsplit_placement():
    place = np.zeros((N_FSPLIT, LANES, FOX_HEADS * LANES), np.float32)
    for p in range(N_FSPLIT):
        for h in range(FOX_HEADS):
            place[p, h, h * LANES + HEAD_DIM + p] = 1.0
    return jnp.asarray(place, BF16)


def kernel(x, c, norm_mix_g, norm_ffn_g, w_ada, b_ada, w_in, b_forget, sinks, w_branch_a,
           w_branch_b, w_out, rel_bias, w_query, sub_keys, expert_down, expert_up, final_norm_g):
    bsz, seq, d = x.shape
    depth = w_ada.shape[0]
    t = bsz * seq
    tm = _row_tile(seq, 512)
    assert seq % FOX_TK == 0 and tm % FOX_TK == 0

    dist = np.arange(BLOCK)[:, None] + BLOCK - np.arange(2 * BLOCK)[None, :]
    in_band = (dist >= 0) & (dist < WINDOW)
    bucket_t = jnp.asarray(np.where(in_band, _t5_bucket(dist), -1).T, jnp.int32)
    place = _fsplit_placement()

    o_a = SWA_WIDTH + 2 * SWA_KV_WIDTH
    o_q, o_k, o_v = o_a, o_a + FOX_WIDTH, o_a + 2 * FOX_WIDTH
    o_b = o_a + 3 * FOX_WIDTH
    o_f = o_b + FOX_HEADS
    pad_head = LANES - HEAD_DIM

    xc = x.reshape(t, d)
    for l in range(depth):
        mod = _ada(c, w_ada[l], b_ada[l])
        sh1, sc1, gt1, sh2, sc2, gt2 = [m.reshape(bsz, 1, d) for m in jnp.split(mod, N_MOD, axis=-1)]

        w = w_in[l]
        wqa3 = w[:, :SWA_WIDTH].T.reshape(SWA_Q_HEADS, HEAD_DIM, d)
        wqat = jnp.pad(wqa3, ((0, 0), (0, pad_head), (0, 0))).reshape(SWA_Q_HEADS * LANES, d).astype(BF16)
        wka3 = w[:, SWA_WIDTH:SWA_WIDTH + SWA_KV_WIDTH].reshape(d, SWA_KV_HEADS, HEAD_DIM)
        wka = jnp.pad(wka3, ((0, 0), (0, 0), (0, pad_head))).reshape(d, SWA_KV_HEADS * LANES).astype(BF16)
        wvat = w[:, SWA_WIDTH + SWA_KV_WIDTH:o_a].T.astype(BF16)
        wq3 = w[:, o_q:o_k].T.reshape(FOX_HEADS, HEAD_DIM, d)
        wqt = jnp.pad(wq3, ((0, 0), (0, pad_head), (0, 0))).reshape(FOX_HEADS * LANES, d).astype(BF16)
        wk3 = w[:, o_k:o_v].reshape(d, FOX_HEADS, HEAD_DIM)
        wk = jnp.pad(wk3, ((0, 0), (0, 0), (0, pad_head))).reshape(d, FOX_HEADS * LANES).astype(BF16)
        wvt = w[:, o_v:o_b].T.astype(BF16)
        wf = jnp.pad(w[:, o_b:o_f], ((0, 0), (0, LANES - FOX_HEADS)))
        wg = w[:, o_f:].astype(BF16)
        qa, ka, va, qt, kpad, vt, fl, ga, gb = _inproj(
            xc, norm_mix_g[l].reshape(1, d), sc1, sh1, wqat, wka, wvat, wqt, wk, wvt, wf, wg, seq, tm)

        bpad = jnp.pad(b_forget[l], (0, LANES - FOX_HEADS)).reshape(1, LANES)
        kf = _kf(fl.reshape(bsz, seq, LANES), bpad, kpad, place)

        oa = _swa(qa, ka, va, sinks[l], rel_bias.astype(F32), bucket_t, bsz, seq)
        ob = _fox(qt, kf, vt, bsz, seq, _row_tile(seq, 512))

        x1, h2t, q = _mix(xc, oa, ob, ga, gb, gt1, norm_ffn_g[l].reshape(1, d), sc2, sh2,
                          w_branch_a[l].astype(BF16), w_branch_b[l].astype(BF16),
                          w_out[l].astype(BF16), w_query[l].astype(BF16), seq, tm)

        e2, r2, e1, rk = _route(q, sub_keys[l], _row_tile(seq, 256))
        ni = 8
        peer_t = _experts(h2t, _pack_bf16_rows(expert_down[l]), _pack_bf16_rows(expert_up[l], transpose=True),
                          e2, r2, e1, rk, _row_tile(seq, EXPERT_TQ), ni)

        g_next = final_norm_g.reshape(1, d)
        xc = _final(x1, peer_t, gt2, g_next, seq, tm)
        assert depth == 1
    return xc.reshape(bsz, seq, d)
```

```python
import functools
import math

import numpy as np
import jax
import jax.numpy as jnp
from jax import lax
from jax.experimental import pallas as pl
from jax.experimental.pallas import tpu as pltpu

HEAD_DIM = 64
SWA_Q_HEADS = 8
SWA_KV_HEADS = 2
SWA_GROUP = SWA_Q_HEADS // SWA_KV_HEADS
WINDOW = 128
BLOCK = 128
FOX_HEADS = 8
SWA_WIDTH = SWA_Q_HEADS * HEAD_DIM
SWA_KV_WIDTH = SWA_KV_HEADS * HEAD_DIM
FOX_WIDTH = FOX_HEADS * HEAD_DIM
NUM_BUCKETS = 32
MAX_DISTANCE = 128
PEER_HEADS = 8
PEER_TOPK = 16
PEER_HALF = 64
N_MOD = 6
EPS = 1e-6
NEG = -1e30
LANES = 128
BF16_SUBLANES = 16
VMEM_LIMIT = 56 * 1024 * 1024
LOG2E = math.log2(math.e)
N_FSPLIT = 3
FOX_TK = 256
SWA_STEP_BLOCKS = 2
FOX_AHEAD = 1
MXU_PIECE_ROWS = 512
MXU_PIECE_COLS = 256
EXPERT_TQ = 1024
IL_GROUP = 1

F32 = jnp.float32
BF16 = jnp.bfloat16


def _cparams(sem):
    return pltpu.CompilerParams(dimension_semantics=sem, vmem_limit_bytes=VMEM_LIMIT)


def _dot(a, b):
    return jnp.dot(a, b, preferred_element_type=F32)


def _dot_nt(a, b):
    return lax.dot_general(a, b, (((1,), (1,)), ((), ())), preferred_element_type=F32)


def _rms_mod(x, g, sc, sh):
    ms = jnp.mean(x * x, axis=-1, keepdims=True)
    return (x * lax.rsqrt(ms + EPS) * g) * (1.0 + sc) + sh


def _ada_kernel(c_ref, w_ref, b_ref, o_ref):
    c = c_ref[...]
    ca = c * jax.nn.sigmoid(c)
    o_ref[...] = _dot(ca, w_ref[...]) + b_ref[...]


def _ada(c, w_ada, b_ada):
    bsz, d = c.shape
    n = w_ada.shape[1]
    tn = d
    return pl.pallas_call(
        _ada_kernel,
        grid=(n // tn,),
        in_specs=[pl.BlockSpec((bsz, d), lambda j: (0, 0)),
                  pl.BlockSpec((d, tn), lambda j: (0, j)),
                  pl.BlockSpec((1, tn), lambda j: (0, j))],
        out_specs=pl.BlockSpec((bsz, tn), lambda j: (0, j)),
        out_shape=jax.ShapeDtypeStruct((bsz, n), F32),
        compiler_params=_cparams(("arbitrary",)),
        name="ada",
    )(c, w_ada, b_ada.reshape(1, n))


def _inproj_kernel(x_ref, g_ref, sc_ref, sh_ref, wqat_ref, wka_ref, wvat_ref, wqt_ref, wk_ref, wvt_ref,
                   wf_ref, wg_ref,
                   qa_ref, ka_ref, va_ref, qt_ref, kp_ref, vt_ref, fl_ref, ga_ref, gb_ref):
    h = _rms_mod(x_ref[...], g_ref[...], sc_ref[0], sh_ref[0])
    hb = h.astype(BF16)
    ht = h.T.astype(BF16)
    scale = HEAD_DIM ** -0.5
    qa_ref[...] = (_dot(wqat_ref[...], ht) * scale).astype(BF16).reshape(qa_ref.shape)
    ka_ref[...] = _dot(hb, wka_ref[...]).astype(BF16)
    va_ref[...] = _dot(wvat_ref[...], ht).astype(BF16)
    qt = _dot(wqt_ref[...], ht) * (scale * LOG2E)
    r = lax.broadcasted_iota(jnp.int32, qt.shape, 0) % LANES
    qt = jnp.where((r >= HEAD_DIM) & (r < HEAD_DIM + N_FSPLIT), 1.0, qt)
    qt_ref[...] = qt.astype(BF16).reshape(qt_ref.shape)
    kp_ref[...] = _dot(hb, wk_ref[...]).astype(BF16)
    vt = _dot(wvt_ref[...], ht).astype(BF16)
    tk = vt_ref.shape[2]
    for cb in range(vt_ref.shape[0]):
        vt_ref[cb] = vt[:, cb * tk:(cb + 1) * tk]
    fl_ref[...] = _dot(h, wf_ref[...])
    d = ga_ref.shape[1]
    pg = _dot(hb, wg_ref[...])
    ga_ref[...] = jax.nn.sigmoid(pg[:, :d]).astype(BF16)
    gb_ref[...] = jax.nn.sigmoid(pg[:, d:]).astype(BF16)


def _inproj(x2d, g, sc1, sh1, wqat, wka, wvat, wqt, wk, wvt, wf, wg, seq, tm):
    t, d = x2d.shape
    per_b = seq // tm
    hp = FOX_HEADS * LANES
    row = lambda w: pl.BlockSpec((tm, w), lambda i: (i, 0))
    col = lambda r: pl.BlockSpec((r, tm), lambda i: (0, i))
    full = lambda a: pl.BlockSpec(a.shape, lambda i: (0,) * a.ndim)
    mod = pl.BlockSpec((1, 1, d), lambda i: (i // per_b, 0, 0))
    out_shape = [jax.ShapeDtypeStruct((SWA_Q_HEADS, HEAD_DIM, t), BF16),
                 jax.ShapeDtypeStruct((t, SWA_KV_HEADS * LANES), BF16),
                 jax.ShapeDtypeStruct((SWA_KV_WIDTH, t), BF16),
                 jax.ShapeDtypeStruct((FOX_HEADS, LANES, t), BF16),
                 jax.ShapeDtypeStruct((t, hp), BF16),
                 jax.ShapeDtypeStruct((t // FOX_TK, FOX_WIDTH, FOX_TK), BF16),
                 jax.ShapeDtypeStruct((t, LANES), F32),
                 jax.ShapeDtypeStruct((t, d), BF16), jax.ShapeDtypeStruct((t, d), BF16)]
    out_specs = [pl.BlockSpec((SWA_Q_HEADS, HEAD_DIM, tm), lambda i: (0, 0, i)),
                 row(SWA_KV_HEADS * LANES), col(SWA_KV_WIDTH),
                 pl.BlockSpec((FOX_HEADS, LANES, tm), lambda i: (0, 0, i)),
                 row(hp),
                 pl.BlockSpec((tm // FOX_TK, FOX_WIDTH, FOX_TK), lambda i: (i, 0, 0)),
                 row(LANES), row(d), row(d)]
    return pl.pallas_call(
        _inproj_kernel,
        grid=(t // tm,),
        in_specs=[row(d), full(g), mod, mod, full(wqat), full(wka), full(wvat), full(wqt), full(wk), full(wvt),
                  full(wf), full(wg)],
        out_specs=out_specs,
        out_shape=out_shape,
        compiler_params=_cparams(("parallel",)),
        name="inproj",
    )(x2d, g, sc1, sh1, wqat, wka, wvat, wqt, wk, wvt, wf, wg)


def _kf_kernel(fl_ref, b_ref, kp_ref, place_ref, ko_ref):
    z = fl_ref[0] + b_ref[...]
    lf = jnp.minimum(z, 0.0) * LOG2E - jnp.log2(1.0 + jnp.exp(-jnp.abs(z)))
    s = lf.shape[0]
    rows = lax.broadcasted_iota(jnp.int32, lf.shape, 0)
    acc = lf
    k = 1
    while k < s:
        shifted = pltpu.roll(acc, k, 0)
        acc = acc + jnp.where(rows >= k, shifted, 0.0)
        k *= 2
    rem = -acc
    placed = None
    for piece in range(N_FSPLIT):
        part = rem.astype(BF16)
        rem = rem - part.astype(F32)
        y = _dot(part, place_ref[piece])
        placed = y if placed is None else placed + y
    ko_ref[...] = (kp_ref[...].astype(F32) + placed).astype(BF16)


def _kf(fl3, bpad, kpad, place):
    bsz, s, _ = fl3.shape
    hp = kpad.shape[1]
    return pl.pallas_call(
        _kf_kernel,
        grid=(bsz,),
        in_specs=[pl.BlockSpec((1, s, LANES), lambda b: (b, 0, 0)),
                  pl.BlockSpec((1, LANES), lambda b: (0, 0)),
                  pl.BlockSpec((s, hp), lambda b: (b, 0)),
                  pl.BlockSpec(place.shape, lambda b: (0, 0, 0))],
        out_specs=pl.BlockSpec((s, hp), lambda b: (b, 0)),
        out_shape=jax.ShapeDtypeStruct(kpad.shape, BF16),
        compiler_params=_cparams(("parallel",)),
        name="kf",
    )(fl3, bpad, kpad, place)


def _swa_kernel(sink_ref, rel_ref, q_ref, kp_ref, kc_ref, vp_ref, vc_ref, bucket_ref, o_ref,
                bias_ref, sinkrow_ref, ot_ref):
    n = pl.program_id(1)

    @pl.when((pl.program_id(0) == 0) & (n == 0))
    def _():
        bucket = bucket_ref[...]
        for h in range(SWA_Q_HEADS):
            tile = jnp.full(bucket.shape, NEG, F32)
            for bk in range(NUM_BUCKETS):
                tile = jnp.where(bucket == bk, rel_ref[bk, h], tile)
            g, hl = divmod(h, SWA_GROUP)
            bias_ref[g, :, hl * BLOCK:(hl + 1) * BLOCK] = tile
            sinkrow_ref[g, :, hl * BLOCK:(hl + 1) * BLOCK] = jnp.full((8, BLOCK), sink_ref[h], F32)

    def block(first):
        assert SWA_STEP_BLOCKS == 2
        kcur = kc_ref[...]
        kbands =[jnp.concatenate([kp_ref[...], kcur[:BLOCK]], axis=0), kcur]
        chains = [(sb, g) for sb in range(SWA_STEP_BLOCKS) for g in range(SWA_KV_HEADS)]

        def scores(sb, g):
            qg = jnp.concatenate([q_ref[g * SWA_GROUP + hl, :, sb * BLOCK:(sb + 1) * BLOCK]
                                  for hl in range(SWA_GROUP)], axis=1)
            return _dot(kbands[sb][:, g * LANES:g * LANES + HEAD_DIM], qg)

        def weighted_values(sb, g, p, denom):
            rows_v = slice(g * HEAD_DIM, (g + 1) * HEAD_DIM)
            if sb == 0:
                vg = jnp.concatenate([vp_ref[rows_v, :], vc_ref[rows_v, :BLOCK]], axis=1)
            else:
                vg = vc_ref[rows_v, :]
            o = _dot(vg, p) / denom
            for hl in range(SWA_GROUP):
                ot_ref[sb, (g * SWA_GROUP + hl) * HEAD_DIM:(g * SWA_GROUP + hl + 1) * HEAD_DIM, :] = (
                    o[:, hl * BLOCK:(hl + 1) * BLOCK])

        s_next = scores(*chains[0])
        pending = None
        for ci, (sb, g) in enumerate(chains):
            s = s_next + bias_ref[g]
            if ci + 1 < len(chains):
                s_next = scores(*chains[ci + 1])
            if first and sb == 0:
                rows = lax.broadcasted_iota(jnp.int32, s.shape, 0)
                s = jnp.where(rows >= BLOCK, s, NEG)
            sink = sinkrow_ref[g, 0:1, :]
            m = jnp.maximum(jnp.max(s, axis=0, keepdims=True), sink)
            p = jnp.exp(s - m)
            denom = jnp.sum(p, axis=0, keepdims=True) + jnp.exp(sink - m)
            if pending is not None:
                weighted_values(*pending)
            pending = (sb, g, p.astype(BF16), denom)
        weighted_values(*pending)
        for sb in range(SWA_STEP_BLOCKS):
            o_ref[sb * BLOCK:(sb + 1) * BLOCK, :] = ot_ref[sb].T.astype(o_ref.dtype)

    @pl.when(n == 0)
    def _():
        block(True)

    @pl.when(n > 0)
    def _():
        block(False)


def _swa(qat, kap, vat, sinks, rel_bias, bucket_t, bsz, seq):
    sbk = SWA_STEP_BLOCKS
    assert seq % (sbk * BLOCK) == 0
    nb = seq // BLOCK
    ns = nb // sbk
    t = bsz * seq
    step = sbk * BLOCK
    cur_r = lambda b, n: (b * ns + n, 0)
    prev_r = lambda b, n: (b * nb + jnp.maximum(sbk * n - 1, 0), 0)
    cur_c = lambda b, n: (0, b * ns + n)
    prev_c = lambda b, n: (0, b * nb + jnp.maximum(sbk * n - 1, 0))
    kw = SWA_KV_HEADS * LANES
    return pl.pallas_call(
        _swa_kernel,
        grid=(bsz, ns),
        in_specs=[pl.BlockSpec(memory_space=pltpu.SMEM),
                  pl.BlockSpec(memory_space=pltpu.SMEM),
                  pl.BlockSpec((SWA_Q_HEADS, HEAD_DIM, step), lambda b, n: (0, 0, b * ns + n)),
                  pl.BlockSpec((BLOCK, kw), prev_r),
                  pl.BlockSpec((step, kw), cur_r),
                  pl.BlockSpec((SWA_KV_WIDTH, BLOCK), prev_c),
                  pl.BlockSpec((SWA_KV_WIDTH, step), cur_c),
                  pl.BlockSpec(bucket_t.shape, lambda b, n: (0, 0))],
        out_specs=pl.BlockSpec((step, SWA_WIDTH), cur_r),
        out_shape=jax.ShapeDtypeStruct((t, SWA_WIDTH), BF16),
        scratch_shapes=[pltpu.VMEM((SWA_KV_HEADS, 2 * BLOCK, SWA_GROUP * BLOCK), F32),
                        pltpu.VMEM((SWA_KV_HEADS, 8, SWA_GROUP * BLOCK), F32),
                        pltpu.VMEM((SWA_STEP_BLOCKS, SWA_WIDTH, BLOCK), F32)],
        compiler_params=_cparams(("arbitrary", "arbitrary")),
        name="swa",
    )(sinks, rel_bias, qat, kap, kap, vat, vat, bucket_t)


def _fox_kernel(q_ref, k_ref, v_ref, o_ref, m_ref, l_ref, acc_ref, *, tq, tk):
    qi = pl.program_id(1)
    m_ref[...] = jnp.full(m_ref.shape, NEG, F32)
    l_ref[...] = jnp.zeros(l_ref.shape, F32)
    acc_ref[...] = jnp.zeros(acc_ref.shape, F32)
    per_q = tq // tk

    def block(j, masked):
        krow0 = pl.multiple_of(j * tk, tk)
        if masked:
            kpos = j * tk + lax.broadcasted_iota(jnp.int32, (tk, tq), 0)
            qpos = qi * tq + lax.broadcasted_iota(jnp.int32, (tk, tq), 1)
            causal = kpos <= qpos

        def scores(h):
            kblk = k_ref[pl.ds(krow0, tk), h * LANES:(h + 1) * LANES]
            return _dot(kblk, q_ref[h])

        def weighted_values(h, p, alpha):
            rows = slice(h * HEAD_DIM, (h + 1) * HEAD_DIM)
            pv = _dot(v_ref[j, rows, :], p)
            acc_ref[rows, :] = alpha * acc_ref[rows, :] + pv

        ahead = [scores(h) for h in range(FOX_AHEAD)]
        pending = None
        for h in range(FOX_HEADS):
            s = ahead.pop(0)
            if h + FOX_AHEAD < FOX_HEADS:
                ahead.append(scores(h + FOX_AHEAD))
            if masked:
                s = jnp.where(causal, s, NEG)
            m_old = m_ref[h:h + 1, :]
            m_new = jnp.maximum(m_old, jnp.max(s, axis=0, keepdims=True))
            alpha = jnp.exp2(m_old - m_new)
            p = jnp.exp2(s - m_new)
            l_ref[h:h + 1, :] = alpha * l_ref[h:h + 1, :] + jnp.sum(p, axis=0, keepdims=True)
            m_ref[h:h + 1, :] = m_new
            if pending is not None:
                weighted_values(*pending)
            pending = (h, p.astype(BF16), alpha)
        weighted_values(*pending)

    def body(j, carry):
        block(j, False)
        return carry

    lax.fori_loop(0, qi * per_q, body, 0)
    for jj in range(per_q):
        block(qi * per_q + jj, True)

    outs = []
    for h in range(FOX_HEADS):
        rows = slice(h * HEAD_DIM, (h + 1) * HEAD_DIM)
        outs.append(acc_ref[rows, :] / l_ref[h:h + 1, :])
    o_ref[...] = jnp.concatenate(outs, axis=0).T.astype(o_ref.dtype)


def _fox(qt, kf, vt, bsz, seq, tq):
    tk = FOX_TK
    nq = seq // tq
    hp = kf.shape[1]
    return pl.pallas_call(
        functools.partial(_fox_kernel, tq=tq, tk=tk),
        grid=(bsz, nq),
        in_specs=[pl.BlockSpec((FOX_HEADS, LANES, tq), lambda b, i: (0, 0, b * nq + i)),
                  pl.BlockSpec((seq, hp), lambda b, i: (b, 0)),
                  pl.BlockSpec((seq // tk, FOX_WIDTH, tk), lambda b, i: (b, 0, 0))],
        out_specs=pl.BlockSpec((tq, FOX_WIDTH), lambda b, i: (b * nq + i, 0)),
        out_shape=jax.ShapeDtypeStruct((bsz * seq, FOX_WIDTH), BF16),
        scratch_shapes=[pltpu.VMEM((FOX_HEADS, tq), F32),
                        pltpu.VMEM((FOX_HEADS, tq), F32),
                        pltpu.VMEM((FOX_WIDTH, tq), F32)],
        compiler_params=_cparams(("parallel", "arbitrary")),
        name="fox",
    )(qt, kf, vt)


def _mix_kernel(x_ref, oa_ref, ob_ref, ga_ref, gb_ref, gt1_ref, g2_ref, sc2_ref, sh2_ref,
                wa_ref, wb_ref, wo_ref, wq_ref, x1_ref, h2t_ref, q_ref):
    ya = _dot(oa_ref[...], wa_ref[...])
    yb = _dot(ob_ref[...], wb_ref[...])
    mixed = ga_ref[...].astype(F32) * ya + gb_ref[...].astype(F32) * yb
    x1 = x_ref[...] + gt1_ref[0] * _dot(mixed.astype(BF16), wo_ref[...])
    x1_ref[...] = x1
    h2 = _rms_mod(x1, g2_ref[...], sc2_ref[0], sh2_ref[0])
    h2t_ref[...] = h2.T.astype(BF16)
    q_ref[...] = _dot(h2.astype(BF16), wq_ref[...])


def _mix(x2d, oa, ob, ga, gb, gt1, g2, sc2, sh2, wa, wb, wo, wq, seq, tm):
    t, d = x2d.shape
    per_b = seq // tm
    row = lambda w: pl.BlockSpec((tm, w), lambda i: (i, 0))
    full = lambda a: pl.BlockSpec(a.shape, lambda i: (0,) * a.ndim)
    mod = pl.BlockSpec((1, 1, d), lambda i: (i // per_b, 0, 0))
    return pl.pallas_call(
        _mix_kernel,
        grid=(t // tm,),
        in_specs=[row(d), row(SWA_WIDTH), row(FOX_WIDTH), row(d), row(d), mod, full(g2), mod, mod,
                  full(wa), full(wb), full(wo), full(wq)],
        out_specs=[row(d), pl.BlockSpec((d, tm), lambda i: (0, i)), row(wq.shape[1])],
        out_shape=[jax.ShapeDtypeStruct((t, d), F32), jax.ShapeDtypeStruct((d, t), BF16),
                   jax.ShapeDtypeStruct((t, wq.shape[1]), F32)],
        compiler_params=_cparams(("parallel",)),
        name="mix",
    )(x2d, oa, ob, ga, gb, gt1, g2, sc2, sh2, wa, wb, wo, wq)


def _top_desc(vals, count):
    rowidx = lax.broadcasted_iota(jnp.int32, vals.shape, 0).astype(F32)
    tops = []
    work = vals
    for _ in range(count):
        mx = jnp.max(work, axis=0, keepdims=True)
        first = jnp.min(jnp.where(work == mx, rowidx, float(vals.shape[0])), axis=0, keepdims=True)
        tops.append(mx)
        work = jnp.where(rowidx == first, -jnp.inf, work)
    return tops, work


def _sort_network(n):
    def merge(lo, hi, r):
        step = r * 2
        if step < hi - lo:
            yield from merge(lo, hi, step)
            yield from merge(lo + r, hi, step)
            yield from [(i, i + r) for i in range(lo + r, hi - r, step)]
        else:
            yield (lo, lo + r)

    def sort(lo, hi):
        if hi - lo >= 1:
            mid = lo + (hi - lo) // 2
            yield from sort(lo, mid)
            yield from sort(mid + 1, hi)
            yield from merge(lo, hi, 1)

    return list(sort(0, n - 1))


def _top_desc_sorted(vals, count):
    n = vals.shape[0]
    depth = n // 8
    lists = [vals[8 * r:8 * (r + 1), :] for r in range(depth)]
    for a, b in _sort_network(depth):
        hi, lo = jnp.maximum(lists[a], lists[b]), jnp.minimum(lists[a], lists[b])
        lists[a], lists[b] = hi, lo
    tops = []
    for k in range(count):
        mx = jnp.max(lists[0], axis=0, keepdims=True)
        tops.append(mx)
        need = min(depth, count - 1 - k)
        if need == 0:
            break
        pop = lists[0] == mx
        for r in range(need):
            nxt = lists[r + 1] if r + 1 < depth else -jnp.inf
            lists[r] = jnp.where(pop, nxt, lists[r])
    return tops


def _route_kernel(q_ref, sk_ref, e2_ref, r2_ref, e1_ref, rk_ref):
    q = q_ref[...]
    nk = sk_ref.shape[2]
    sorted_ok = nk % 8 == 0 and (nk // 8) & (nk // 8 - 1) == 0
    top = _top_desc_sorted if sorted_ok else (lambda v, n: _top_desc(v, n)[0])
    k = PEER_TOPK
    for h in range(PEER_HEADS):
        q1 = q[:, (2 * h) * PEER_HALF:(2 * h + 1) * PEER_HALF]
        q2 = q[:, (2 * h + 1) * PEER_HALF:(2 * h + 2) * PEER_HALF]
        s1 = _dot_nt(sk_ref[h, 0], q1)
        s2 = _dot_nt(sk_ref[h, 1], q2)
        a = top(s1, k)
        b = top(s2, k)
        c, n1 = _top_pair_sums(a, b, k)
        z = jnp.zeros_like(c[0])
        for r in range(k):
            z = z + jnp.exp(c[r] - c[0])
        rk = jnp.zeros(s2.shape, F32)
        cnt = jnp.zeros(s1.shape, F32)
        for r in range(k):
            rk = jnp.where(b[r] > s2, float(r + 1), rk)
        for r in reversed(range(k)):
            cnt = jnp.where(s1 == a[r], n1[r], cnt)
        e2_ref[h] = pltpu.bitcast(jnp.exp(s2 - b[0]).astype(BF16), jnp.uint32)
        r2_ref[h] = pltpu.bitcast(rk.astype(BF16), jnp.uint32)
        e1_ref[h] = jnp.exp(s1 - a[0]) * (0.5 / z)
        rk_ref[h] = cnt


def _top_pair_sums(a, b, k):
    assert k == 2 * 8, "two sublane groups of lists"
    t = a[0].shape[1]
    a_lo = jnp.concatenate(a[:8], axis=0)
    a_hi = jnp.concatenate(a[8:], axis=0)
    sub = lax.broadcasted_iota(jnp.int32, (8, t), 0)
    idx_lo = sub.astype(F32)
    idx_hi = idx_lo + 8.0
    neg = jnp.full((8, t), -jnp.inf, F32)
    depth = [a_lo + b[0]] + [jnp.where(sub < k // (d + 1), a_lo + b[d], neg) for d in range(1, k)]
    head_hi = a_hi + b[0]
    n_lo = jnp.zeros((8, t), F32)
    n_hi = jnp.zeros((8, t), F32)
    sums = []
    for step in range(k):
        mx = jnp.max(jnp.maximum(depth[0], head_hi), axis=0, keepdims=True)
        sums.append(mx)
        first = jnp.min(jnp.minimum(jnp.where(depth[0] == mx, idx_lo, float(k)),
                                    jnp.where(head_hi == mx, idx_hi, float(k))), axis=0, keepdims=True)
        pop_lo = idx_lo == first
        pop_hi = idx_hi == first
        n_lo = n_lo + jnp.where(pop_lo, 1.0, 0.0)
        n_hi = n_hi + jnp.where(pop_hi, 1.0, 0.0)
        left = k - 1 - step
        for d in range(min(left, k - 1)):
            nxt = depth[d + 1] if d + 1 < k else neg
            depth[d] = jnp.where(pop_lo, nxt, depth[d])
        head_hi = jnp.where(pop_hi, neg, head_hi)
    n1 = [n_lo[r:r + 1, :] for r in range(8)] + [n_hi[r:r + 1, :] for r in range(8)]
    return sums, n1


def _route(q, sub_keys, tr):
    t, w = q.shape
    nk = sub_keys.shape[2]
    spec = pl.BlockSpec((PEER_HEADS, nk, tr), lambda i: (0, 0, i))
    shp = jax.ShapeDtypeStruct((PEER_HEADS, nk, t), F32)
    spec_j = pl.BlockSpec((PEER_HEADS, nk // 2, tr), lambda i: (0, 0, i))
    shp_j = jax.ShapeDtypeStruct((PEER_HEADS, nk // 2, t), jnp.uint32)
    return pl.pallas_call(
        _route_kernel,
        grid=(t // tr,),
        in_specs=[pl.BlockSpec((tr, w), lambda i: (i, 0)),
                  pl.BlockSpec(sub_keys.shape, lambda i: (0, 0, 0, 0))],
        out_specs=[spec_j, spec_j, spec, spec],
        out_shape=[shp_j, shp_j, shp, shp],
        compiler_params=_cparams(("parallel",)),
        name="route",
    )(q, sub_keys)


def _experts_kernel(h2t_ref, dn_ref, upt_ref, e2_ref, r2_ref, e1_ref, rk_ref, o_ref, act_ref, g_ref,
                    *, nk, ni, tq, ne, n_items):
    s = pl.program_id(0)
    c_item = jnp.clip(s - 2, 0, n_items - 1)

    @pl.when(s == 0)
    def _():
        act_ref[...] = jnp.zeros(act_ref.shape, F32)
        g_ref[...] = jnp.zeros(g_ref.shape, BF16)

    @pl.when(c_item % ne == 0)
    def _():
        o_ref[...] = jnp.zeros(o_ref.shape, F32)

    slot_a = s % 2
    slot_b = (s + 1) % 2
    et = ni * nk
    d = o_ref.shape[0]
    inv_sqrt2 = 2.0 ** -0.5
    mrow = min(MXU_PIECE_ROWS, et, d)
    ncol = min(MXU_PIECE_COLS, tq)

    def piece_a(mr, nc):
        rows, cols = slice(mr * mrow, (mr + 1) * mrow), slice(nc * ncol, (nc + 1) * ncol)
        words = slice(mr * mrow // 2, (mr + 1) * mrow // 2)
        act_ref[slot_a, rows, cols] = _dot(pltpu.bitcast(dn_ref[words, :], BF16), h2t_ref[:, cols])

    def piece_c(mr, nc):
        rows, cols = slice(mr * mrow, (mr + 1) * mrow), slice(nc * ncol, (nc + 1) * ncol)
        words = slice(mr * mrow // 2, (mr + 1) * mrow // 2)
        o_ref[rows, cols] += _dot(pltpu.bitcast(upt_ref[words, :], BF16), g_ref[slot_a, :, cols])

    def block_b(il0, c):
        cols = slice(c * LANES, (c + 1) * LANES)
        sub = BF16_SUBLANES
        ils = range(il0, il0 + IL_GROUP)
        w = [[None] * (nk // sub) for _ in ils]
        for h in range(PEER_HEADS):
            e1b = [jnp.broadcast_to(e1_ref[h, il:il + 1, cols], (sub, LANES)).astype(BF16) for il in ils]
            rkb = [jnp.broadcast_to(rk_ref[h, il:il + 1, cols], (sub, LANES)).astype(BF16) for il in ils]
            for k in range(nk // sub):
                wr = slice(k * sub // 2, (k + 1) * sub // 2)
                e2v = pltpu.bitcast(e2_ref[h, wr, cols], BF16)
                r2v = pltpu.bitcast(r2_ref[h, wr, cols], BF16)
                for u in range(IL_GROUP):
                    p = e2v * e1b[u]
                    sel = jnp.where(r2v < rkb[u], p, jnp.zeros_like(p))
                    w[u][k] = sel if w[u][k] is None else w[u][k] + sel
        for u, il in enumerate(ils):
            for k in range(nk // sub):
                jr = slice(il * nk + k * sub, il * nk + (k + 1) * sub)
                x = act_ref[slot_b, jr, cols]
                g_ref[slot_b, jr, cols] = (x * (1.0 + lax.erf(x * inv_sqrt2))).astype(BF16) * w[u][k]

    pieces = []
    for nc in range(tq // ncol):
        for mr in range(max(et, d) // mrow):
            if mr < et // mrow:
                pieces.append(functools.partial(piece_a, mr, nc))
            if mr < d // mrow:
                pieces.append(functools.partial(piece_c, mr, nc))
    blocks = [(il0, c) for il0 in range(0, ni, IL_GROUP) for c in range(tq // LANES)]
    lead = min(2, len(pieces))
    for k in range(lead):
        pieces[k]()
    per_piece = max(1, len(blocks) // max(1, len(pieces) - lead + 1))
    nxt = lead
    for k, (il, c) in enumerate(blocks):
        block_b(il, c)
        if (k + 1) % per_piece == 0 and nxt < len(pieces):
            pieces[nxt]()
            nxt += 1
    for k in range(nxt, len(pieces)):
        pieces[k]()


def _experts(h2t, dn, upt, e2, r2, e1, rk, tq, ni):
    d, t = h2t.shape
    n_exp = dn.shape[0] * 2
    nk = e1.shape[1]
    et = ni * nk
    ne = n_exp // et
    n_items = (t // tq) * ne
    a_item = lambda s: jnp.minimum(s, n_items - 1)
    b_item = lambda s: jnp.clip(s - 1, 0, n_items - 1)
    c_item = lambda s: jnp.clip(s - 2, 0, n_items - 1)
    per_j = pl.BlockSpec((PEER_HEADS, nk // 2, tq), lambda s: (0, 0, b_item(s) // ne))
    per_i = pl.BlockSpec((PEER_HEADS, ni, tq), lambda s: (0, b_item(s) % ne, b_item(s) // ne))
    return pl.pallas_call(
        functools.partial(_experts_kernel, nk=nk, ni=ni, tq=tq, ne=ne, n_items=n_items),
        grid=(n_items + 2,),
        in_specs=[pl.BlockSpec((d, tq), lambda s: (0, a_item(s) // ne)),
                  pl.BlockSpec((et // 2, d), lambda s: (a_item(s) % ne, 0)),
                  pl.BlockSpec((d // 2, et), lambda s: (0, c_item(s) % ne)),
                  per_j, per_j, per_i, per_i],
        out_specs=pl.BlockSpec((d, tq), lambda s: (0, c_item(s) // ne)),
        out_shape=jax.ShapeDtypeStruct((d, t), F32),
        scratch_shapes=[pltpu.VMEM((2, et, tq), F32), pltpu.VMEM((2, et, tq), BF16)],
        compiler_params=_cparams(("arbitrary",)),
        name="experts",
    )(h2t, dn, upt, e2, r2, e1, rk)


def _final_kernel(x1_ref, pt_ref, gt2_ref, g_ref, o_ref):
    x2 = x1_ref[...] + gt2_ref[0] * pt_ref[...].T
    ms = jnp.mean(x2 * x2, axis=-1, keepdims=True)
    o_ref[...] = x2 * lax.rsqrt(ms + EPS) * g_ref[...]


def _final(x1, peer_t, gt2, g, seq, tm):
    t, d = x1.shape
    per_b = seq // tm
    return pl.pallas_call(
        _final_kernel,
        grid=(t // tm,),
        in_specs=[pl.BlockSpec((tm, d), lambda i: (i, 0)),
                  pl.BlockSpec((d, tm), lambda i: (0, i)),
                  pl.BlockSpec((1, 1, d), lambda i: (i // per_b, 0, 0)),
                  pl.BlockSpec((1, d), lambda i: (0, 0))],
        out_specs=pl.BlockSpec((tm, d), lambda i: (i, 0)),
        out_shape=jax.ShapeDtypeStruct((t, d), F32),
        compiler_params=_cparams(("parallel",)),
        name="final",
    )(x1, peer_t, gt2, g)


def _t5_bucket(dist):
    dist = np.clip(dist, 0, None)
    max_exact = NUM_BUCKETS // 2
    large = max_exact + (np.log(np.maximum(dist, 1) / max_exact) / np.log(MAX_DISTANCE / max_exact)
                         * (NUM_BUCKETS - max_exact)).astype(np.int32)
    large = np.minimum(large, NUM_BUCKETS - 1)
    return np.where(dist < max_exact, dist, large).astype(np.int32)


def _pack_kernel(x_ref, o_ref, *, transpose):
    x = x_ref[...]
    if transpose:
        x = x.T
    o_ref[...] = pltpu.bitcast(x.astype(BF16), jnp.uint32)


def _pack_bf16_rows(x, transpose=False, tile=512):
    r, c = x.shape
    if transpose:
        in_spec = pl.BlockSpec((tile, c), lambda i: (i, 0))
        out_spec = pl.BlockSpec((c // 2, tile), lambda i: (0, i))
        out_shape = jax.ShapeDtypeStruct((c // 2, r), jnp.uint32)
    else:
        in_spec = pl.BlockSpec((tile, c), lambda i: (i, 0))
        out_spec = pl.BlockSpec((tile // 2, c), lambda i: (i, 0))
        out_shape = jax.ShapeDtypeStruct((r // 2, c), jnp.uint32)
    return pl.pallas_call(
        functools.partial(_pack_kernel, transpose=transpose),
        grid=(r // tile,),
        in_specs=[in_spec],
        out_specs=out_spec,
        out_shape=out_shape,
        compiler_params=_cparams(("parallel",)),
        name="pack_t" if transpose else "pack",
    )(x)


def _row_tile(seq, want):
    tm = min(want, seq)
    assert seq % tm == 0
    return tm


def _fsplit_placement():
    place = np.zeros((N_FSPLIT, LANES, FOX_HEADS * LANES), np.float32)
    for p in range(N_FSPLIT):
        for h in range(FOX_HEADS):
            place[p, h, h * LANES + HEAD_DIM + p] = 1.0
    return jnp.asarray(place, BF16)


def kernel(x, c, norm_mix_g, norm_ffn_g, w_ada, b_ada, w_in, b_forget, sinks, w_branch_a,
           w_branch_b, w_out, rel_bias, w_query, sub_keys, expert_down, expert_up, final_norm_g):
    bsz, seq, d = x.shape
    depth = w_ada.shape[0]
    t = bsz * seq
    tm = _row_tile(seq, 512)
    assert seq % FOX_TK == 0 and tm % FOX_TK == 0

    dist = np.arange(BLOCK)[:, None] + BLOCK - np.arange(2 * BLOCK)[None, :]
    in_band = (dist >= 0) & (dist < WINDOW)
    bucket_t = jnp.asarray(np.where(in_band, _t5_bucket(dist), -1).T, jnp.int32)
    place = _fsplit_placement()

    o_a = SWA_WIDTH + 2 * SWA_KV_WIDTH
    o_q, o_k, o_v = o_a, o_a + FOX_WIDTH, o_a + 2 * FOX_WIDTH
    o_b = o_a + 3 * FOX_WIDTH
    o_f = o_b + FOX_HEADS
    pad_head = LANES - HEAD_DIM

    xc = x.reshape(t, d)
    for l in range(depth):
        mod = _ada(c, w_ada[l], b_ada[l])
        sh1, sc1, gt1, sh2, sc2, gt2 = [m.reshape(bsz, 1, d) for m in jnp.split(mod, N_MOD, axis=-1)]

        w = w_in[l]
        wqat = w[:, :SWA_WIDTH].T.astype(BF16)
        wka3 = w[:, SWA_WIDTH:SWA_WIDTH + SWA_KV_WIDTH].reshape(d, SWA_KV_HEADS, HEAD_DIM)
        wka = jnp.pad(wka3, ((0, 0), (0, 0), (0, pad_head))).reshape(d, SWA_KV_HEADS * LANES).astype(BF16)
        wvat = w[:, SWA_WIDTH + SWA_KV_WIDTH:o_a].T.astype(BF16)
        wq3 = w[:, o_q:o_k].T.reshape(FOX_HEADS, HEAD_DIM, d)
        wqt = jnp.pad(wq3, ((0, 0), (0, pad_head), (0, 0))).reshape(FOX_HEADS * LANES, d).astype(BF16)
        wk3 = w[:, o_k:o_v].reshape(d, FOX_HEADS, HEAD_DIM)
        wk = jnp.pad(wk3, ((0, 0), (0, 0), (0, pad_head))).reshape(d, FOX_HEADS * LANES).astype(BF16)
        wvt = w[:, o_v:o_b].T.astype(BF16)
        wf = jnp.pad(w[:, o_b:o_f], ((0, 0), (0, LANES - FOX_HEADS)))
        wg = w[:, o_f:].astype(BF16)
        qa, ka, va, qt, kpad, vt, fl, ga, gb = _inproj(
            xc, norm_mix_g[l].reshape(1, d), sc1, sh1, wqat, wka, wvat, wqt, wk, wvt, wf, wg, seq, tm)

        bpad = jnp.pad(b_forget[l], (0, LANES - FOX_HEADS)).reshape(1, LANES)
        kf = _kf(fl.reshape(bsz, seq, LANES), bpad, kpad, place)

        oa = _swa(qa, ka, va, sinks[l], rel_bias.astype(F32), bucket_t, bsz, seq)
        ob = _fox(qt, kf, vt, bsz, seq, _row_tile(seq, 512))

        x1, h2t, q = _mix(xc, oa, ob, ga, gb, gt1, norm_ffn_g[l].reshape(1, d), sc2, sh2,
                          w_branch_a[l].astype(BF16), w_branch_b[l].astype(BF16),
                          w_out[l].astype(BF16), w_query[l].astype(BF16), seq, tm)

        e2, r2, e1, rk = _route(q, sub_keys[l], _row_tile(seq, 256))
        ni = 8
        peer_t = _experts(h2t, _pack_bf16_rows(expert_down[l]), _pack_bf16_rows(expert_up[l], transpose=True),
                          e2, r2, e1, rk, _row_tile(seq, EXPERT_TQ), ni)

        g_next = final_norm_g.reshape(1, d)
        xc = _final(x1, peer_t, gt2, g_next, seq, tm)
        assert depth == 1
    return xc.reshape(bsz, seq, d)
```

```python
import functools
import math

import numpy as np
import jax
import jax.numpy as jnp
from jax import lax
from jax.experimental import pallas as pl
from jax.experimental.pallas import tpu as pltpu

HEAD_DIM = 64
SWA_Q_HEADS = 8
SWA_KV_HEADS = 2
SWA_GROUP = SWA_Q_HEADS // SWA_KV_HEADS
WINDOW = 128
BLOCK = 128
FOX_HEADS = 8
SWA_WIDTH = SWA_Q_HEADS * HEAD_DIM
SWA_KV_WIDTH = SWA_KV_HEADS * HEAD_DIM
FOX_WIDTH = FOX_HEADS * HEAD_DIM
NUM_BUCKETS = 32
MAX_DISTANCE = 128
PEER_HEADS = 8
PEER_TOPK = 16
PEER_HALF = 64
N_MOD = 6
EPS = 1e-6
NEG = -1e30
LANES = 128
BF16_SUBLANES = 16
VMEM_LIMIT = 56 * 1024 * 1024
LOG2E = math.log2(math.e)
N_FSPLIT = 3
FOX_TK = 256
SWA_STEP_BLOCKS = 2
FOX_AHEAD = 1
MXU_PIECE_ROWS = 512
MXU_PIECE_COLS = 256
EXPERT_TQ = 1024
EXPERT_LEAD_PIECES = 4

F32 = jnp.float32
BF16 = jnp.bfloat16


def _cparams(sem):
    return pltpu.CompilerParams(dimension_semantics=sem, vmem_limit_bytes=VMEM_LIMIT)


def _dot(a, b):
    return jnp.dot(a, b, preferred_element_type=F32)


def _dot_nt(a, b):
    return lax.dot_general(a, b, (((1,), (1,)), ((), ())), preferred_element_type=F32)


def _rms_mod(x, g, sc, sh):
    ms = jnp.mean(x * x, axis=-1, keepdims=True)
    return (x * lax.rsqrt(ms + EPS) * g) * (1.0 + sc) + sh


def _ada_kernel(c_ref, w_ref, b_ref, o_ref):
    c = c_ref[...]
    ca = c * jax.nn.sigmoid(c)
    o_ref[...] = _dot(ca, w_ref[...]) + b_ref[...]


def _ada(c, w_ada, b_ada):
    bsz, d = c.shape
    n = w_ada.shape[1]
    tn = d
    return pl.pallas_call(
        _ada_kernel,
        grid=(n // tn,),
        in_specs=[pl.BlockSpec((bsz, d), lambda j: (0, 0)),
                  pl.BlockSpec((d, tn), lambda j: (0, j)),
                  pl.BlockSpec((1, tn), lambda j: (0, j))],
        out_specs=pl.BlockSpec((bsz, tn), lambda j: (0, j)),
        out_shape=jax.ShapeDtypeStruct((bsz, n), F32),
        compiler_params=_cparams(("arbitrary",)),
        name="ada",
    )(c, w_ada, b_ada.reshape(1, n))


def _inproj_kernel(x_ref, g_ref, sc_ref, sh_ref, wqat_ref, wka_ref, wvat_ref, wqt_ref, wk_ref, wvt_ref,
                   wf_ref, wg_ref,
                   qa_ref, ka_ref, va_ref, qt_ref, kp_ref, vt_ref, fl_ref, ga_ref, gb_ref):
    h = _rms_mod(x_ref[...], g_ref[...], sc_ref[0], sh_ref[0])
    hb = h.astype(BF16)
    ht = h.T.astype(BF16)
    scale = HEAD_DIM ** -0.5
    qa_ref[...] = (_dot(wqat_ref[...], ht) * scale).astype(BF16).reshape(qa_ref.shape)
    ka_ref[...] = _dot(hb, wka_ref[...]).astype(BF16)
    va_ref[...] = _dot(wvat_ref[...], ht).astype(BF16)
    qt = _dot(wqt_ref[...], ht) * (scale * LOG2E)
    r = lax.broadcasted_iota(jnp.int32, qt.shape, 0) % LANES
    qt = jnp.where((r >= HEAD_DIM) & (r < HEAD_DIM + N_FSPLIT), 1.0, qt)
    qt_ref[...] = qt.astype(BF16).reshape(qt_ref.shape)
    kp_ref[...] = _dot(hb, wk_ref[...]).astype(BF16)
    vt = _dot(wvt_ref[...], ht).astype(BF16)
    tk = vt_ref.shape[2]
    for cb in range(vt_ref.shape[0]):
        vt_ref[cb] = vt[:, cb * tk:(cb + 1) * tk]
    fl_ref[...] = _dot(h, wf_ref[...])
    d = ga_ref.shape[1]
    pg = _dot(hb, wg_ref[...])
    ga_ref[...] = jax.nn.sigmoid(pg[:, :d]).astype(BF16)
    gb_ref[...] = jax.nn.sigmoid(pg[:, d:]).astype(BF16)


def _inproj(x2d, g, sc1, sh1, wqat, wka, wvat, wqt, wk, wvt, wf, wg, seq, tm):
    t, d = x2d.shape
    per_b = seq // tm
    hp = FOX_HEADS * LANES
    row = lambda w: pl.BlockSpec((tm, w), lambda i: (i, 0))
    col = lambda r: pl.BlockSpec((r, tm), lambda i: (0, i))
    full = lambda a: pl.BlockSpec(a.shape, lambda i: (0,) * a.ndim)
    mod = pl.BlockSpec((1, 1, d), lambda i: (i // per_b, 0, 0))
    out_shape = [jax.ShapeDtypeStruct((SWA_Q_HEADS, HEAD_DIM, t), BF16),
                 jax.ShapeDtypeStruct((t, SWA_KV_HEADS * LANES), BF16),
                 jax.ShapeDtypeStruct((SWA_KV_WIDTH, t), BF16),
                 jax.ShapeDtypeStruct((FOX_HEADS, LANES, t), BF16),
                 jax.ShapeDtypeStruct((t, hp), BF16),
                 jax.ShapeDtypeStruct((t // FOX_TK, FOX_WIDTH, FOX_TK), BF16),
                 jax.ShapeDtypeStruct((t, LANES), F32),
                 jax.ShapeDtypeStruct((t, d), BF16), jax.ShapeDtypeStruct((t, d), BF16)]
    out_specs = [pl.BlockSpec((SWA_Q_HEADS, HEAD_DIM, tm), lambda i: (0, 0, i)),
                 row(SWA_KV_HEADS * LANES), col(SWA_KV_WIDTH),
                 pl.BlockSpec((FOX_HEADS, LANES, tm), lambda i: (0, 0, i)),
                 row(hp),
                 pl.BlockSpec((tm // FOX_TK, FOX_WIDTH, FOX_TK), lambda i: (i, 0, 0)),
                 row(LANES), row(d), row(d)]
    return pl.pallas_call(
        _inproj_kernel,
        grid=(t // tm,),
        in_specs=[row(d), full(g), mod, mod, full(wqat), full(wka), full(wvat), full(wqt), full(wk), full(wvt),
                  full(wf), full(wg)],
        out_specs=out_specs,
        out_shape=out_shape,
        compiler_params=_cparams(("parallel",)),
        name="inproj",
    )(x2d, g, sc1, sh1, wqat, wka, wvat, wqt, wk, wvt, wf, wg)


def _kf_kernel(fl_ref, b_ref, kp_ref, place_ref, ko_ref):
    z = fl_ref[0] + b_ref[...]
    lf = jnp.minimum(z, 0.0) * LOG2E - jnp.log2(1.0 + jnp.exp(-jnp.abs(z)))
    s = lf.shape[0]
    rows = lax.broadcasted_iota(jnp.int32, lf.shape, 0)
    acc = lf
    k = 1
    while k < s:
        shifted = pltpu.roll(acc, k, 0)
        acc = acc + jnp.where(rows >= k, shifted, 0.0)
        k *= 2
    rem = -acc
    placed = None
    for piece in range(N_FSPLIT):
        part = rem.astype(BF16)
        rem = rem - part.astype(F32)
        y = _dot(part, place_ref[piece])
        placed = y if placed is None else placed + y
    ko_ref[...] = (kp_ref[...].astype(F32) + placed).astype(BF16)


def _kf(fl3, bpad, kpad, place):
    bsz, s, _ = fl3.shape
    hp = kpad.shape[1]
    return pl.pallas_call(
        _kf_kernel,
        grid=(bsz,),
        in_specs=[pl.BlockSpec((1, s, LANES), lambda b: (b, 0, 0)),
                  pl.BlockSpec((1, LANES), lambda b: (0, 0)),
                  pl.BlockSpec((s, hp), lambda b: (b, 0)),
                  pl.BlockSpec(place.shape, lambda b: (0, 0, 0))],
        out_specs=pl.BlockSpec((s, hp), lambda b: (b, 0)),
        out_shape=jax.ShapeDtypeStruct(kpad.shape, BF16),
        compiler_params=_cparams(("parallel",)),
        name="kf",
    )(fl3, bpad, kpad, place)


def _swa_kernel(sink_ref, rel_ref, q_ref, kp_ref, kc_ref, vp_ref, vc_ref, bucket_ref, o_ref,
                bias_ref, sinkrow_ref, ot_ref):
    n = pl.program_id(1)

    @pl.when((pl.program_id(0) == 0) & (n == 0))
    def _():
        bucket = bucket_ref[...]
        for h in range(SWA_Q_HEADS):
            tile = jnp.full(bucket.shape, NEG, F32)
            for bk in range(NUM_BUCKETS):
                tile = jnp.where(bucket == bk, rel_ref[bk, h], tile)
            g, hl = divmod(h, SWA_GROUP)
            bias_ref[g, :, hl * BLOCK:(hl + 1) * BLOCK] = tile
            sinkrow_ref[g, :, hl * BLOCK:(hl + 1) * BLOCK] = jnp.full((8, BLOCK), sink_ref[h], F32)

    def block(first):
        assert SWA_STEP_BLOCKS == 2
        kcur = kc_ref[...]
        kbands =[jnp.concatenate([kp_ref[...], kcur[:BLOCK]], axis=0), kcur]
        chains = [(sb, g) for sb in range(SWA_STEP_BLOCKS) for g in range(SWA_KV_HEADS)]

        def scores(sb, g):
            qg = jnp.concatenate([q_ref[g * SWA_GROUP + hl, :, sb * BLOCK:(sb + 1) * BLOCK]
                                  for hl in range(SWA_GROUP)], axis=1)
            return _dot(kbands[sb][:, g * LANES:g * LANES + HEAD_DIM], qg)

        def weighted_values(sb, g, p, denom):
            rows_v = slice(g * HEAD_DIM, (g + 1) * HEAD_DIM)
            if sb == 0:
                vg = jnp.concatenate([vp_ref[rows_v, :], vc_ref[rows_v, :BLOCK]], axis=1)
            else:
                vg = vc_ref[rows_v, :]
            o = _dot(vg, p) / denom
            for hl in range(SWA_GROUP):
                ot_ref[sb, (g * SWA_GROUP + hl) * HEAD_DIM:(g * SWA_GROUP + hl + 1) * HEAD_DIM, :] = (
                    o[:, hl * BLOCK:(hl + 1) * BLOCK])

        s_next = scores(*chains[0])
        pending = None
        for ci, (sb, g) in enumerate(chains):
            s = s_next + bias_ref[g]
            if ci + 1 < len(chains):
                s_next = scores(*chains[ci + 1])
            if first and sb == 0:
                rows = lax.broadcasted_iota(jnp.int32, s.shape, 0)
                s = jnp.where(rows >= BLOCK, s, NEG)
            sink = sinkrow_ref[g, 0:1, :]
            m = jnp.maximum(jnp.max(s, axis=0, keepdims=True), sink)
            p = jnp.exp(s - m)
            denom = jnp.sum(p, axis=0, keepdims=True) + jnp.exp(sink - m)
            if pending is not None:
                weighted_values(*pending)
            pending = (sb, g, p.astype(BF16), denom)
        weighted_values(*pending)
        for sb in range(SWA_STEP_BLOCKS):
            o_ref[sb * BLOCK:(sb + 1) * BLOCK, :] = ot_ref[sb].T.astype(o_ref.dtype)

    @pl.when(n == 0)
    def _():
        block(True)

    @pl.when(n > 0)
    def _():
        block(False)


def _swa(qat, kap, vat, sinks, rel_bias, bucket_t, bsz, seq):
    sbk = SWA_STEP_BLOCKS
    assert seq % (sbk * BLOCK) == 0
    nb = seq // BLOCK
    ns = nb // sbk
    t = bsz * seq
    step = sbk * BLOCK
    cur_r = lambda b, n: (b * ns + n, 0)
    prev_r = lambda b, n: (b * nb + jnp.maximum(sbk * n - 1, 0), 0)
    cur_c = lambda b, n: (0, b * ns + n)
    prev_c = lambda b, n: (0, b * nb + jnp.maximum(sbk * n - 1, 0))
    kw = SWA_KV_HEADS * LANES
    return pl.pallas_call(
        _swa_kernel,
        grid=(bsz, ns),
        in_specs=[pl.BlockSpec(memory_space=pltpu.SMEM),
                  pl.BlockSpec(memory_space=pltpu.SMEM),
                  pl.BlockSpec((SWA_Q_HEADS, HEAD_DIM, step), lambda b, n: (0, 0, b * ns + n)),
                  pl.BlockSpec((BLOCK, kw), prev_r),
                  pl.BlockSpec((step, kw), cur_r),
                  pl.BlockSpec((SWA_KV_WIDTH, BLOCK), prev_c),
                  pl.BlockSpec((SWA_KV_WIDTH, step), cur_c),
                  pl.BlockSpec(bucket_t.shape, lambda b, n: (0, 0))],
        out_specs=pl.BlockSpec((step, SWA_WIDTH), cur_r),
        out_shape=jax.ShapeDtypeStruct((t, SWA_WIDTH), BF16),
        scratch_shapes=[pltpu.VMEM((SWA_KV_HEADS, 2 * BLOCK, SWA_GROUP * BLOCK), F32),
                        pltpu.VMEM((SWA_KV_HEADS, 8, SWA_GROUP * BLOCK), F32),
                        pltpu.VMEM((SWA_STEP_BLOCKS, SWA_WIDTH, BLOCK), F32)],
        compiler_params=_cparams(("arbitrary", "arbitrary")),
        name="swa",
    )(sinks, rel_bias, qat, kap, kap, vat, vat, bucket_t)


def _fox_kernel(q_ref, k_ref, v_ref, o_ref, m_ref, l_ref, acc_ref, *, tq, tk):
    qi = pl.program_id(1)
    m_ref[...] = jnp.full(m_ref.shape, NEG, F32)
    l_ref[...] = jnp.zeros(l_ref.shape, F32)
    acc_ref[...] = jnp.zeros(acc_ref.shape, F32)
    per_q = tq // tk

    def block(j, masked):
        krow0 = pl.multiple_of(j * tk, tk)
        if masked:
            kpos = j * tk + lax.broadcasted_iota(jnp.int32, (tk, tq), 0)
            qpos = qi * tq + lax.broadcasted_iota(jnp.int32, (tk, tq), 1)
            causal = kpos <= qpos

        def scores(h):
            kblk = k_ref[pl.ds(krow0, tk), h * LANES:(h + 1) * LANES]
            return _dot(kblk, q_ref[h])

        def weighted_values(h, p, alpha):
            rows = slice(h * HEAD_DIM, (h + 1) * HEAD_DIM)
            pv = _dot(v_ref[j, rows, :], p)
            acc_ref[rows, :] = alpha * acc_ref[rows, :] + pv

        ahead = [scores(h) for h in range(FOX_AHEAD)]
        pending = None
        for h in range(FOX_HEADS):
            s = ahead.pop(0)
            if h + FOX_AHEAD < FOX_HEADS:
                ahead.append(scores(h + FOX_AHEAD))
            if masked:
                s = jnp.where(causal, s, NEG)
            m_old = m_ref[h:h + 1, :]
            m_new = jnp.maximum(m_old, jnp.max(s, axis=0, keepdims=True))
            alpha = jnp.exp2(m_old - m_new)
            p = jnp.exp2(s - m_new)
            l_ref[h:h + 1, :] = alpha * l_ref[h:h + 1, :] + jnp.sum(p, axis=0, keepdims=True)
            m_ref[h:h + 1, :] = m_new
            if pending is not None:
                weighted_values(*pending)
            pending = (h, p.astype(BF16), alpha)
        weighted_values(*pending)

    def body(j, carry):
        block(j, False)
        return carry

    lax.fori_loop(0, qi * per_q, body, 0)
    for jj in range(per_q):
        block(qi * per_q + jj, True)

    outs = []
    for h in range(FOX_HEADS):
        rows = slice(h * HEAD_DIM, (h + 1) * HEAD_DIM)
        outs.append(acc_ref[rows, :] / l_ref[h:h + 1, :])
    o_ref[...] = jnp.concatenate(outs, axis=0).T.astype(o_ref.dtype)


def _fox(qt, kf, vt, bsz, seq, tq):
    tk = FOX_TK
    nq = seq // tq
    hp = kf.shape[1]
    return pl.pallas_call(
        functools.partial(_fox_kernel, tq=tq, tk=tk),
        grid=(bsz, nq),
        in_specs=[pl.BlockSpec((FOX_HEADS, LANES, tq), lambda b, i: (0, 0, b * nq + i)),
                  pl.BlockSpec((seq, hp), lambda b, i: (b, 0)),
                  pl.BlockSpec((seq // tk, FOX_WIDTH, tk), lambda b, i: (b, 0, 0))],
        out_specs=pl.BlockSpec((tq, FOX_WIDTH), lambda b, i: (b * nq + i, 0)),
        out_shape=jax.ShapeDtypeStruct((bsz * seq, FOX_WIDTH), BF16),
        scratch_shapes=[pltpu.VMEM((FOX_HEADS, tq), F32),
                        pltpu.VMEM((FOX_HEADS, tq), F32),
                        pltpu.VMEM((FOX_WIDTH, tq), F32)],
        compiler_params=_cparams(("parallel", "arbitrary")),
        name="fox",
    )(qt, kf, vt)


def _mix_kernel(x_ref, oa_ref, ob_ref, ga_ref, gb_ref, gt1_ref, g2_ref, sc2_ref, sh2_ref,
                wa_ref, wb_ref, wo_ref, wq_ref, x1_ref, h2t_ref, q_ref):
    ya = _dot(oa_ref[...], wa_ref[...])
    yb = _dot(ob_ref[...], wb_ref[...])
    mixed = ga_ref[...].astype(F32) * ya + gb_ref[...].astype(F32) * yb
    x1 = x_ref[...] + gt1_ref[0] * _dot(mixed.astype(BF16), wo_ref[...])
    x1_ref[...] = x1
    h2 = _rms_mod(x1, g2_ref[...], sc2_ref[0], sh2_ref[0])
    h2t_ref[...] = h2.T.astype(BF16)
    q_ref[...] = _dot(h2.astype(BF16), wq_ref[...])


def _mix(x2d, oa, ob, ga, gb, gt1, g2, sc2, sh2, wa, wb, wo, wq, seq, tm):
    t, d = x2d.shape
    per_b = seq // tm
    row = lambda w: pl.BlockSpec((tm, w), lambda i: (i, 0))
    full = lambda a: pl.BlockSpec(a.shape, lambda i: (0,) * a.ndim)
    mod = pl.BlockSpec((1, 1, d), lambda i: (i // per_b, 0, 0))
    return pl.pallas_call(
        _mix_kernel,
        grid=(t // tm,),
        in_specs=[row(d), row(SWA_WIDTH), row(FOX_WIDTH), row(d), row(d), mod, full(g2), mod, mod,
                  full(wa), full(wb), full(wo), full(wq)],
        out_specs=[row(d), pl.BlockSpec((d, tm), lambda i: (0, i)), row(wq.shape[1])],
        out_shape=[jax.ShapeDtypeStruct((t, d), F32), jax.ShapeDtypeStruct((d, t), BF16),
                   jax.ShapeDtypeStruct((t, wq.shape[1]), F32)],
        compiler_params=_cparams(("parallel",)),
        name="mix",
    )(x2d, oa, ob, ga, gb, gt1, g2, sc2, sh2, wa, wb, wo, wq)


def _top_desc(vals, count):
    rowidx = lax.broadcasted_iota(jnp.int32, vals.shape, 0).astype(F32)
    tops = []
    work = vals
    for _ in range(count):
        mx = jnp.max(work, axis=0, keepdims=True)
        first = jnp.min(jnp.where(work == mx, rowidx, float(vals.shape[0])), axis=0, keepdims=True)
        tops.append(mx)
        work = jnp.where(rowidx == first, -jnp.inf, work)
    return tops, work


def _sort_network(n):
    def merge(lo, hi, r):
        step = r * 2
        if step < hi - lo:
            yield from merge(lo, hi, step)
            yield from merge(lo + r, hi, step)
            yield from [(i, i + r) for i in range(lo + r, hi - r, step)]
        else:
            yield (lo, lo + r)

    def sort(lo, hi):
        if hi - lo >= 1:
            mid = lo + (hi - lo) // 2
            yield from sort(lo, mid)
            yield from sort(mid + 1, hi)
            yield from merge(lo, hi, 1)

    return list(sort(0, n - 1))


def _top_desc_sorted(vals, count):
    n = vals.shape[0]
    depth = n // 8
    lists = [vals[8 * r:8 * (r + 1), :] for r in range(depth)]
    for a, b in _sort_network(depth):
        hi, lo = jnp.maximum(lists[a], lists[b]), jnp.minimum(lists[a], lists[b])
        lists[a], lists[b] = hi, lo
    tops = []
    for k in range(count):
        mx = jnp.max(lists[0], axis=0, keepdims=True)
        tops.append(mx)
        need = min(depth, count - 1 - k)
        if need == 0:
            break
        pop = lists[0] == mx
        for r in range(need):
            nxt = lists[r + 1] if r + 1 < depth else -jnp.inf
            lists[r] = jnp.where(pop, nxt, lists[r])
    return tops


def _route_kernel(q_ref, sk_ref, e2_ref, r2_ref, e1_ref, rk_ref):
    q = q_ref[...]
    nk = sk_ref.shape[2]
    sorted_ok = nk % 8 == 0 and (nk // 8) & (nk // 8 - 1) == 0
    top = _top_desc_sorted if sorted_ok else (lambda v, n: _top_desc(v, n)[0])
    k = PEER_TOPK
    for h in range(PEER_HEADS):
        q1 = q[:, (2 * h) * PEER_HALF:(2 * h + 1) * PEER_HALF]
        q2 = q[:, (2 * h + 1) * PEER_HALF:(2 * h + 2) * PEER_HALF]
        s1 = _dot_nt(sk_ref[h, 0], q1)
        s2 = _dot_nt(sk_ref[h, 1], q2)
        a = top(s1, k)
        b = top(s2, k)
        c, n1 = _top_pair_sums(a, b, k)
        z = jnp.zeros_like(c[0])
        for r in range(k):
            z = z + jnp.exp(c[r] - c[0])
        rk = jnp.zeros(s2.shape, F32)
        cnt = jnp.zeros(s1.shape, F32)
        for r in range(k):
            rk = jnp.where(b[r] > s2, float(r + 1), rk)
        for r in reversed(range(k)):
            cnt = jnp.where(s1 == a[r], n1[r], cnt)
        e2_ref[h] = pltpu.bitcast(jnp.exp(s2 - b[0]).astype(BF16), jnp.uint32)
        r2_ref[h] = pltpu.bitcast(rk.astype(BF16), jnp.uint32)
        e1_ref[h] = jnp.exp(s1 - a[0]) * (0.5 / z)
        rk_ref[h] = cnt


def _top_pair_sums(a, b, k):
    assert k == 2 * 8, "two sublane groups of lists"
    t = a[0].shape[1]
    a_lo = jnp.concatenate(a[:8], axis=0)
    a_hi = jnp.concatenate(a[8:], axis=0)
    sub = lax.broadcasted_iota(jnp.int32, (8, t), 0)
    idx_lo = sub.astype(F32)
    idx_hi = idx_lo + 8.0
    neg = jnp.full((8, t), -jnp.inf, F32)
    depth = [a_lo + b[0]] + [jnp.where(sub < k // (d + 1), a_lo + b[d], neg) for d in range(1, k)]
    head_hi = a_hi + b[0]
    n_lo = jnp.zeros((8, t), F32)
    n_hi = jnp.zeros((8, t), F32)
    sums = []
    for step in range(k):
        mx = jnp.max(jnp.maximum(depth[0], head_hi), axis=0, keepdims=True)
        sums.append(mx)
        first = jnp.min(jnp.minimum(jnp.where(depth[0] == mx, idx_lo, float(k)),
                                    jnp.where(head_hi == mx, idx_hi, float(k))), axis=0, keepdims=True)
        pop_lo = idx_lo == first
        pop_hi = idx_hi == first
        n_lo = n_lo + jnp.where(pop_lo, 1.0, 0.0)
        n_hi = n_hi + jnp.where(pop_hi, 1.0, 0.0)
        left = k - 1 - step
        for d in range(min(left, k - 1)):
            nxt = depth[d + 1] if d + 1 < k else neg
            depth[d] = jnp.where(pop_lo, nxt, depth[d])
        head_hi = jnp.where(pop_hi, neg, head_hi)
    n1 = [n_lo[r:r + 1, :] for r in range(8)] + [n_hi[r:r + 1, :] for r in range(8)]
    return sums, n1


def _route(q, sub_keys, tr):
    t, w = q.shape
    nk = sub_keys.shape[2]
    spec = pl.BlockSpec((PEER_HEADS, nk, tr), lambda i: (0, 0, i))
    shp = jax.ShapeDtypeStruct((PEER_HEADS, nk, t), F32)
    spec_j = pl.BlockSpec((PEER_HEADS, nk // 2, tr), lambda i: (0, 0, i))
    shp_j = jax.ShapeDtypeStruct((PEER_HEADS, nk // 2, t), jnp.uint32)
    return pl.pallas_call(
        _route_kernel,
        grid=(t // tr,),
        in_specs=[pl.BlockSpec((tr, w), lambda i: (i, 0)),
                  pl.BlockSpec(sub_keys.shape, lambda i: (0, 0, 0, 0))],
        out_specs=[spec_j, spec_j, spec, spec],
        out_shape=[shp_j, shp_j, shp, shp],
        compiler_params=_cparams(("parallel",)),
        name="route",
    )(q, sub_keys)


def _experts_kernel(h2t_ref, dn_ref, upt_ref, e2_ref, r2_ref, e1_ref, rk_ref, o_ref, act_ref, g_ref,
                    *, nk, ni, tq, ne, n_items):
    s = pl.program_id(0)
    c_item = jnp.clip(s - 2, 0, n_items - 1)

    @pl.when(s == 0)
    def _():
        act_ref[...] = jnp.zeros(act_ref.shape, F32)
        g_ref[...] = jnp.zeros(g_ref.shape, BF16)

    @pl.when(c_item % ne == 0)
    def _():
        o_ref[...] = jnp.zeros(o_ref.shape, F32)

    slot_a = s % 2
    slot_b = (s + 1) % 2
    et = ni * nk
    d = o_ref.shape[0]
    inv_sqrt2 = 2.0 ** -0.5
    mrow = min(MXU_PIECE_ROWS, et, d)
    ncol = min(MXU_PIECE_COLS, tq)

    def piece_a(mr, nc):
        rows, cols = slice(mr * mrow, (mr + 1) * mrow), slice(nc * ncol, (nc + 1) * ncol)
        words = slice(mr * mrow // 2, (mr + 1) * mrow // 2)
        act_ref[slot_a, rows, cols] = _dot(pltpu.bitcast(dn_ref[words, :], BF16), h2t_ref[:, cols])

    def piece_c(mr, nc):
        rows, cols = slice(mr * mrow, (mr + 1) * mrow), slice(nc * ncol, (nc + 1) * ncol)
        words = slice(mr * mrow // 2, (mr + 1) * mrow // 2)
        o_ref[rows, cols] += _dot(pltpu.bitcast(upt_ref[words, :], BF16), g_ref[slot_a, :, cols])

    def block_b(il, c):
        cols = slice(c * LANES, (c + 1) * LANES)
        sub = BF16_SUBLANES
        e1b = [jnp.broadcast_to(e1_ref[h, il:il + 1, cols], (sub, LANES)).astype(BF16) for h in range(PEER_HEADS)]
        rkb = [jnp.broadcast_to(rk_ref[h, il:il + 1, cols], (sub, LANES)).astype(BF16) for h in range(PEER_HEADS)]
        for k in range(nk // sub):
            wr = slice(k * sub // 2, (k + 1) * sub // 2)
            w = None
            for h in range(PEER_HEADS):
                p = pltpu.bitcast(e2_ref[h, wr, cols], BF16) * e1b[h]
                sel = jnp.where(pltpu.bitcast(r2_ref[h, wr, cols], BF16) < rkb[h], p, jnp.zeros_like(p))
                w = sel if w is None else w + sel
            jr = slice(il * nk + k * sub, il * nk + (k + 1) * sub)
            x = act_ref[slot_b, jr, cols]
            g_ref[slot_b, jr, cols] = (x * (1.0 + lax.erf(x * inv_sqrt2))).astype(BF16) * w

    pieces = []
    for nc in range(tq // ncol):
        for mr in range(max(et, d) // mrow):
            if mr < et // mrow:
                pieces.append(functools.partial(piece_a, mr, nc))
            if mr < d // mrow:
                pieces.append(functools.partial(piece_c, mr, nc))
    blocks = [(il, c) for il in range(ni) for c in range(tq // LANES)]
    lead = min(EXPERT_LEAD_PIECES, len(pieces))
    for k in range(lead):
        pieces[k]()
    per_piece = max(1, (len(blocks) // 2) // max(1, len(pieces) - lead + 1))
    nxt = lead
    for k, (il, c) in enumerate(blocks):
        block_b(il, c)
        if (k + 1) % per_piece == 0 and nxt < len(pieces):
            pieces[nxt]()
            nxt += 1
    for k in range(nxt, len(pieces)):
        pieces[k]()


def _experts(h2t, dn, upt, e2, r2, e1, rk, tq, ni):
    d, t = h2t.shape
    n_exp = dn.shape[0] * 2
    nk = e1.shape[1]
    et = ni * nk
    ne = n_exp // et
    n_items = (t // tq) * ne
    a_item = lambda s: jnp.minimum(s, n_items - 1)
    b_item = lambda s: jnp.clip(s - 1, 0, n_items - 1)
    c_item = lambda s: jnp.clip(s - 2, 0, n_items - 1)
    per_j = pl.BlockSpec((PEER_HEADS, nk // 2, tq), lambda s: (0, 0, b_item(s) // ne))
    per_i = pl.BlockSpec((PEER_HEADS, ni, tq), lambda s: (0, b_item(s) % ne, b_item(s) // ne))
    return pl.pallas_call(
        functools.partial(_experts_kernel, nk=nk, ni=ni, tq=tq, ne=ne, n_items=n_items),
        grid=(n_items + 2,),
        in_specs=[pl.BlockSpec((d, tq), lambda s: (0, a_item(s) // ne)),
                  pl.BlockSpec((et // 2, d), lambda s: (a_item(s) % ne, 0)),
                  pl.BlockSpec((d // 2, et), lambda s: (0, c_item(s) % ne)),
                  per_j, per_j, per_i, per_i],
        out_specs=pl.BlockSpec((d, tq), lambda s: (0, c_item(s) // ne)),
        out_shape=jax.ShapeDtypeStruct((d, t), F32),
        scratch_shapes=[pltpu.VMEM((2, et, tq), F32), pltpu.VMEM((2, et, tq), BF16)],
        compiler_params=_cparams(("arbitrary",)),
        name="experts",
    )(h2t, dn, upt, e2, r2, e1, rk)


def _final_kernel(x1_ref, pt_ref, gt2_ref, g_ref, o_ref):
    x2 = x1_ref[...] + gt2_ref[0] * pt_ref[...].T
    ms = jnp.mean(x2 * x2, axis=-1, keepdims=True)
    o_ref[...] = x2 * lax.rsqrt(ms + EPS) * g_ref[...]


def _final(x1, peer_t, gt2, g, seq, tm):
    t, d = x1.shape
    per_b = seq // tm
    return pl.pallas_call(
        _final_kernel,
        grid=(t // tm,),
        in_specs=[pl.BlockSpec((tm, d), lambda i: (i, 0)),
                  pl.BlockSpec((d, tm), lambda i: (0, i)),
                  pl.BlockSpec((1, 1, d), lambda i: (i // per_b, 0, 0)),
                  pl.BlockSpec((1, d), lambda i: (0, 0))],
        out_specs=pl.BlockSpec((tm, d), lambda i: (i, 0)),
        out_shape=jax.ShapeDtypeStruct((t, d), F32),
        compiler_params=_cparams(("parallel",)),
        name="final",
    )(x1, peer_t, gt2, g)


def _t5_bucket(dist):
    dist = np.clip(dist, 0, None)
    max_exact = NUM_BUCKETS // 2
    large = max_exact + (np.log(np.maximum(dist, 1) / max_exact) / np.log(MAX_DISTANCE / max_exact)
                         * (NUM_BUCKETS - max_exact)).astype(np.int32)
    large = np.minimum(large, NUM_BUCKETS - 1)
    return np.where(dist < max_exact, dist, large).astype(np.int32)


def _pack_kernel(x_ref, o_ref, *, transpose):
    x = x_ref[...]
    if transpose:
        x = x.T
    o_ref[...] = pltpu.bitcast(x.astype(BF16), jnp.uint32)


def _pack_bf16_rows(x, transpose=False, tile=512):
    r, c = x.shape
    if transpose:
        in_spec = pl.BlockSpec((tile, c), lambda i: (i, 0))
        out_spec = pl.BlockSpec((c // 2, tile), lambda i: (0, i))
        out_shape = jax.ShapeDtypeStruct((c // 2, r), jnp.uint32)
    else:
        in_spec = pl.BlockSpec((tile, c), lambda i: (i, 0))
        out_spec = pl.BlockSpec((tile // 2, c), lambda i: (i, 0))
        out_shape = jax.ShapeDtypeStruct((r // 2, c), jnp.uint32)
    return pl.pallas_call(
        functools.partial(_pack_kernel, transpose=transpose),
        grid=(r // tile,),
        in_specs=[in_spec],
        out_specs=out_spec,
        out_shape=out_shape,
        compiler_params=_cparams(("parallel",)),
        name="pack_t" if transpose else "pack",
    )(x)


def _row_tile(seq, want):
    tm = min(want, seq)
    assert seq % tm == 0
    return tm


def _fsplit_placement():
    place = np.zeros((N_FSPLIT, LANES, FOX_HEADS * LANES), np.float32)
    for p in range(N_FSPLIT):
        for h in range(FOX_HEADS):
            place[p, h, h * LANES + HEAD_DIM + p] = 1.0
    return jnp.asarray(place, BF16)


def kernel(x, c, norm_mix_g, norm_ffn_g, w_ada, b_ada, w_in, b_forget, sinks, w_branch_a,
           w_branch_b, w_out, rel_bias, w_query, sub_keys, expert_down, expert_up, final_norm_g):
    bsz, seq, d = x.shape
    depth = w_ada.shape[0]
    t = bsz * seq
    tm = _row_tile(seq, 512)
    assert seq % FOX_TK == 0 and tm % FOX_TK == 0

    dist = np.arange(BLOCK)[:, None] + BLOCK - np.arange(2 * BLOCK)[None, :]
    in_band = (dist >= 0) & (dist < WINDOW)
    bucket_t = jnp.asarray(np.where(in_band, _t5_bucket(dist), -1).T, jnp.int32)
    place = _fsplit_placement()

    o_a = SWA_WIDTH + 2 * SWA_KV_WIDTH
    o_q, o_k, o_v = o_a, o_a + FOX_WIDTH, o_a + 2 * FOX_WIDTH
    o_b = o_a + 3 * FOX_WIDTH
    o_f = o_b + FOX_HEADS
    pad_head = LANES - HEAD_DIM

    xc = x.reshape(t, d)
    for l in range(depth):
        mod = _ada(c, w_ada[l], b_ada[l])
        sh1, sc1, gt1, sh2, sc2, gt2 = [m.reshape(bsz, 1, d) for m in jnp.split(mod, N_MOD, axis=-1)]

        w = w_in[l]
        wqat = w[:, :SWA_WIDTH].T.astype(BF16)
        wka3 = w[:, SWA_WIDTH:SWA_WIDTH + SWA_KV_WIDTH].reshape(d, SWA_KV_HEADS, HEAD_DIM)
        wka = jnp.pad(wka3, ((0, 0), (0, 0), (0, pad_head))).reshape(d, SWA_KV_HEADS * LANES).astype(BF16)
        wvat = w[:, SWA_WIDTH + SWA_KV_WIDTH:o_a].T.astype(BF16)
        wq3 = w[:, o_q:o_k].T.reshape(FOX_HEADS, HEAD_DIM, d)
        wqt = jnp.pad(wq3, ((0, 0), (0, pad_head), (0, 0))).reshape(FOX_HEADS * LANES, d).astype(BF16)
        wk3 = w[:, o_k:o_v].reshape(d, FOX_HEADS, HEAD_DIM)
        wk = jnp.pad(wk3, ((0, 0), (0, 0), (0, pad_head))).reshape(d, FOX_HEADS * LANES).astype(BF16)
        wvt = w[:, o_v:o_b].T.astype(BF16)
        wf = jnp.pad(w[:, o_b:o_f], ((0, 0), (0, LANES - FOX_HEADS)))
        wg = w[:, o_f:].astype(BF16)
        qa, ka, va, qt, kpad, vt, fl, ga, gb = _inproj(
            xc, norm_mix_g[l].reshape(1, d), sc1, sh1, wqat, wka, wvat, wqt, wk, wvt, wf, wg, seq, tm)

        bpad = jnp.pad(b_forget[l], (0, LANES - FOX_HEADS)).reshape(1, LANES)
        kf = _kf(fl.reshape(bsz, seq, LANES), bpad, kpad, place)

        oa = _swa(qa, ka, va, sinks[l], rel_bias.astype(F32), bucket_t, bsz, seq)
        ob = _fox(qt, kf, vt, bsz, seq, _row_tile(seq, 512))

        x1, h2t, q = _mix(xc, oa, ob, ga, gb, gt1, norm_ffn_g[l].reshape(1, d), sc2, sh2,
                          w_branch_a[l].astype(BF16), w_branch_b[l].astype(BF16),
                          w_out[l].astype(BF16), w_query[l].astype(BF16), seq, tm)

        e2, r2, e1, rk = _route(q, sub_keys[l], _row_tile(seq, 256))
        ni = 8
        peer_t = _experts(h2t, _pack_bf16_rows(expert_down[l]), _pack_bf16_rows(expert_up[l], transpose=True),
                          e2, r2, e1, rk, _row_tile(seq, EXPERT_TQ), ni)

        g_next = final_norm_g.reshape(1, d)
        xc = _final(x1, peer_t, gt2, g_next, seq, tm)
        assert depth == 1
    return xc.reshape(bsz, seq, d)
```

```python
import functools
import math

import numpy as np
import jax
import jax.numpy as jnp
from jax import lax
from jax.experimental import pallas as pl
from jax.experimental.pallas import tpu as pltpu

HEAD_DIM = 64
SWA_Q_HEADS = 8
SWA_KV_HEADS = 2
SWA_GROUP = SWA_Q_HEADS // SWA_KV_HEADS
WINDOW = 128
BLOCK = 128
FOX_HEADS = 8
SWA_WIDTH = SWA_Q_HEADS * HEAD_DIM
SWA_KV_WIDTH = SWA_KV_HEADS * HEAD_DIM
FOX_WIDTH = FOX_HEADS * HEAD_DIM
NUM_BUCKETS = 32
MAX_DISTANCE = 128
PEER_HEADS = 8
PEER_TOPK = 16
PEER_HALF = 64
N_MOD = 6
EPS = 1e-6
NEG = -1e30
LANES = 128
BF16_SUBLANES = 16
VMEM_LIMIT = 56 * 1024 * 1024
LOG2E = math.log2(math.e)
N_FSPLIT = 3
FOX_TK = 256
SWA_STEP_BLOCKS = 2
FOX_Q_SPLIT = 2
FOX_AHEAD = 1
MXU_PIECE_ROWS = 512
MXU_PIECE_COLS = 256
EXPERT_TQ = 1024
EXPERT_LEAD_PIECES = 4

F32 = jnp.float32
BF16 = jnp.bfloat16


def _cparams(sem):
    return pltpu.CompilerParams(dimension_semantics=sem, vmem_limit_bytes=VMEM_LIMIT)


def _dot(a, b):
    return jnp.dot(a, b, preferred_element_type=F32)


def _dot_nt(a, b):
    return lax.dot_general(a, b, (((1,), (1,)), ((), ())), preferred_element_type=F32)


def _rms_mod(x, g, sc, sh):
    ms = jnp.mean(x * x, axis=-1, keepdims=True)
    return (x * lax.rsqrt(ms + EPS) * g) * (1.0 + sc) + sh


def _ada_kernel(c_ref, w_ref, b_ref, o_ref):
    c = c_ref[...]
    ca = c * jax.nn.sigmoid(c)
    o_ref[...] = _dot(ca, w_ref[...]) + b_ref[...]


def _ada(c, w_ada, b_ada):
    bsz, d = c.shape
    n = w_ada.shape[1]
    tn = d
    return pl.pallas_call(
        _ada_kernel,
        grid=(n // tn,),
        in_specs=[pl.BlockSpec((bsz, d), lambda j: (0, 0)),
                  pl.BlockSpec((d, tn), lambda j: (0, j)),
                  pl.BlockSpec((1, tn), lambda j: (0, j))],
        out_specs=pl.BlockSpec((bsz, tn), lambda j: (0, j)),
        out_shape=jax.ShapeDtypeStruct((bsz, n), F32),
        compiler_params=_cparams(("arbitrary",)),
        name="ada",
    )(c, w_ada, b_ada.reshape(1, n))


def _inproj_kernel(x_ref, g_ref, sc_ref, sh_ref, wqat_ref, wka_ref, wvat_ref, wqt_ref, wk_ref, wvt_ref,
                   wf_ref, wg_ref,
                   qa_ref, ka_ref, va_ref, qt_ref, kp_ref, vt_ref, fl_ref, ga_ref, gb_ref):
    h = _rms_mod(x_ref[...], g_ref[...], sc_ref[0], sh_ref[0])
    hb = h.astype(BF16)
    ht = h.T.astype(BF16)
    scale = HEAD_DIM ** -0.5
    qa_ref[...] = (_dot(wqat_ref[...], ht) * scale).astype(BF16).reshape(qa_ref.shape)
    ka_ref[...] = _dot(hb, wka_ref[...]).astype(BF16)
    va_ref[...] = _dot(wvat_ref[...], ht).astype(BF16)
    qt = _dot(wqt_ref[...], ht) * (scale * LOG2E)
    r = lax.broadcasted_iota(jnp.int32, qt.shape, 0) % LANES
    qt = jnp.where((r >= HEAD_DIM) & (r < HEAD_DIM + N_FSPLIT), 1.0, qt)
    qt_ref[...] = qt.astype(BF16).reshape(qt_ref.shape)
    kp_ref[...] = _dot(hb, wk_ref[...]).astype(BF16)
    vt = _dot(wvt_ref[...], ht).astype(BF16)
    tk = vt_ref.shape[2]
    for cb in range(vt_ref.shape[0]):
        vt_ref[cb] = vt[:, cb * tk:(cb + 1) * tk]
    fl_ref[...] = _dot(h, wf_ref[...])
    d = ga_ref.shape[1]
    pg = _dot(hb, wg_ref[...])
    ga_ref[...] = jax.nn.sigmoid(pg[:, :d]).astype(BF16)
    gb_ref[...] = jax.nn.sigmoid(pg[:, d:]).astype(BF16)


def _inproj(x2d, g, sc1, sh1, wqat, wka, wvat, wqt, wk, wvt, wf, wg, seq, tm):
    t, d = x2d.shape
    per_b = seq // tm
    hp = FOX_HEADS * LANES
    row = lambda w: pl.BlockSpec((tm, w), lambda i: (i, 0))
    col = lambda r: pl.BlockSpec((r, tm), lambda i: (0, i))
    full = lambda a: pl.BlockSpec(a.shape, lambda i: (0,) * a.ndim)
    mod = pl.BlockSpec((1, 1, d), lambda i: (i // per_b, 0, 0))
    out_shape = [jax.ShapeDtypeStruct((SWA_Q_HEADS, HEAD_DIM, t), BF16),
                 jax.ShapeDtypeStruct((t, SWA_KV_HEADS * LANES), BF16),
                 jax.ShapeDtypeStruct((SWA_KV_WIDTH, t), BF16),
                 jax.ShapeDtypeStruct((FOX_HEADS, LANES, t), BF16),
                 jax.ShapeDtypeStruct((t, hp), BF16),
                 jax.ShapeDtypeStruct((t // FOX_TK, FOX_WIDTH, FOX_TK), BF16),
                 jax.ShapeDtypeStruct((t, LANES), F32),
                 jax.ShapeDtypeStruct((t, d), BF16), jax.ShapeDtypeStruct((t, d), BF16)]
    out_specs = [pl.BlockSpec((SWA_Q_HEADS, HEAD_DIM, tm), lambda i: (0, 0, i)),
                 row(SWA_KV_HEADS * LANES), col(SWA_KV_WIDTH),
                 pl.BlockSpec((FOX_HEADS, LANES, tm), lambda i: (0, 0, i)),
                 row(hp),
                 pl.BlockSpec((tm // FOX_TK, FOX_WIDTH, FOX_TK), lambda i: (i, 0, 0)),
                 row(LANES), row(d), row(d)]
    return pl.pallas_call(
        _inproj_kernel,
        grid=(t // tm,),
        in_specs=[row(d), full(g), mod, mod, full(wqat), full(wka), full(wvat), full(wqt), full(wk), full(wvt),
                  full(wf), full(wg)],
        out_specs=out_specs,
        out_shape=out_shape,
        compiler_params=_cparams(("parallel",)),
        name="inproj",
    )(x2d, g, sc1, sh1, wqat, wka, wvat, wqt, wk, wvt, wf, wg)


def _kf_kernel(fl_ref, b_ref, kp_ref, place_ref, ko_ref):
    z = fl_ref[0] + b_ref[...]
    lf = jnp.minimum(z, 0.0) * LOG2E - jnp.log2(1.0 + jnp.exp(-jnp.abs(z)))
    s = lf.shape[0]
    rows = lax.broadcasted_iota(jnp.int32, lf.shape, 0)
    acc = lf
    k = 1
    while k < s:
        shifted = pltpu.roll(acc, k, 0)
        acc = acc + jnp.where(rows >= k, shifted, 0.0)
        k *= 2
    rem = -acc
    placed = None
    for piece in range(N_FSPLIT):
        part = rem.astype(BF16)
        rem = rem - part.astype(F32)
        y = _dot(part, place_ref[piece])
        placed = y if placed is None else placed + y
    ko_ref[...] = (kp_ref[...].astype(F32) + placed).astype(BF16)


def _kf(fl3, bpad, kpad, place):
    bsz, s, _ = fl3.shape
    hp = kpad.shape[1]
    return pl.pallas_call(
        _kf_kernel,
        grid=(bsz,),
        in_specs=[pl.BlockSpec((1, s, LANES), lambda b: (b, 0, 0)),
                  pl.BlockSpec((1, LANES), lambda b: (0, 0)),
                  pl.BlockSpec((s, hp), lambda b: (b, 0)),
                  pl.BlockSpec(place.shape, lambda b: (0, 0, 0))],
        out_specs=pl.BlockSpec((s, hp), lambda b: (b, 0)),
        out_shape=jax.ShapeDtypeStruct(kpad.shape, BF16),
        compiler_params=_cparams(("parallel",)),
        name="kf",
    )(fl3, bpad, kpad, place)


def _swa_kernel(sink_ref, rel_ref, q_ref, kp_ref, kc_ref, vp_ref, vc_ref, bucket_ref, o_ref,
                bias_ref, sinkrow_ref, ot_ref):
    n = pl.program_id(1)

    @pl.when((pl.program_id(0) == 0) & (n == 0))
    def _():
        bucket = bucket_ref[...]
        for h in range(SWA_Q_HEADS):
            tile = jnp.full(bucket.shape, NEG, F32)
            for bk in range(NUM_BUCKETS):
                tile = jnp.where(bucket == bk, rel_ref[bk, h], tile)
            g, hl = divmod(h, SWA_GROUP)
            bias_ref[g, :, hl * BLOCK:(hl + 1) * BLOCK] = tile
            sinkrow_ref[g, :, hl * BLOCK:(hl + 1) * BLOCK] = jnp.full((8, BLOCK), sink_ref[h], F32)

    def block(first):
        assert SWA_STEP_BLOCKS == 2
        kcur = kc_ref[...]
        kbands =[jnp.concatenate([kp_ref[...], kcur[:BLOCK]], axis=0), kcur]
        chains = [(sb, g) for sb in range(SWA_STEP_BLOCKS) for g in range(SWA_KV_HEADS)]

        def scores(sb, g):
            qg = jnp.concatenate([q_ref[g * SWA_GROUP + hl, :, sb * BLOCK:(sb + 1) * BLOCK]
                                  for hl in range(SWA_GROUP)], axis=1)
            return _dot(kbands[sb][:, g * LANES:g * LANES + HEAD_DIM], qg)

        def weighted_values(sb, g, p, denom):
            rows_v = slice(g * HEAD_DIM, (g + 1) * HEAD_DIM)
            if sb == 0:
                vg = jnp.concatenate([vp_ref[rows_v, :], vc_ref[rows_v, :BLOCK]], axis=1)
            else:
                vg = vc_ref[rows_v, :]
            o = _dot(vg, p) / denom
            for hl in range(SWA_GROUP):
                ot_ref[sb, (g * SWA_GROUP + hl) * HEAD_DIM:(g * SWA_GROUP + hl + 1) * HEAD_DIM, :] = (
                    o[:, hl * BLOCK:(hl + 1) * BLOCK])

        s_next = scores(*chains[0])
        pending = None
        for ci, (sb, g) in enumerate(chains):
            s = s_next + bias_ref[g]
            if ci + 1 < len(chains):
                s_next = scores(*chains[ci + 1])
            if first and sb == 0:
                rows = lax.broadcasted_iota(jnp.int32, s.shape, 0)
                s = jnp.where(rows >= BLOCK, s, NEG)
            sink = sinkrow_ref[g, 0:1, :]
            m = jnp.maximum(jnp.max(s, axis=0, keepdims=True), sink)
            p = jnp.exp(s - m)
            denom = jnp.sum(p, axis=0, keepdims=True) + jnp.exp(sink - m)
            if pending is not None:
                weighted_values(*pending)
            pending = (sb, g, p.astype(BF16), denom)
        weighted_values(*pending)
        for sb in range(SWA_STEP_BLOCKS):
            o_ref[sb * BLOCK:(sb + 1) * BLOCK, :] = ot_ref[sb].T.astype(o_ref.dtype)

    @pl.when(n == 0)
    def _():
        block(True)

    @pl.when(n > 0)
    def _():
        block(False)


def _swa(qat, kap, vat, sinks, rel_bias, bucket_t, bsz, seq):
    sbk = SWA_STEP_BLOCKS
    assert seq % (sbk * BLOCK) == 0
    nb = seq // BLOCK
    ns = nb // sbk
    t = bsz * seq
    step = sbk * BLOCK
    cur_r = lambda b, n: (b * ns + n, 0)
    prev_r = lambda b, n: (b * nb + jnp.maximum(sbk * n - 1, 0), 0)
    cur_c = lambda b, n: (0, b * ns + n)
    prev_c = lambda b, n: (0, b * nb + jnp.maximum(sbk * n - 1, 0))
    kw = SWA_KV_HEADS * LANES
    return pl.pallas_call(
        _swa_kernel,
        grid=(bsz, ns),
        in_specs=[pl.BlockSpec(memory_space=pltpu.SMEM),
                  pl.BlockSpec(memory_space=pltpu.SMEM),
                  pl.BlockSpec((SWA_Q_HEADS, HEAD_DIM, step), lambda b, n: (0, 0, b * ns + n)),
                  pl.BlockSpec((BLOCK, kw), prev_r),
                  pl.BlockSpec((step, kw), cur_r),
                  pl.BlockSpec((SWA_KV_WIDTH, BLOCK), prev_c),
                  pl.BlockSpec((SWA_KV_WIDTH, step), cur_c),
                  pl.BlockSpec(bucket_t.shape, lambda b, n: (0, 0))],
        out_specs=pl.BlockSpec((step, SWA_WIDTH), cur_r),
        out_shape=jax.ShapeDtypeStruct((t, SWA_WIDTH), BF16),
        scratch_shapes=[pltpu.VMEM((SWA_KV_HEADS, 2 * BLOCK, SWA_GROUP * BLOCK), F32),
                        pltpu.VMEM((SWA_KV_HEADS, 8, SWA_GROUP * BLOCK), F32),
                        pltpu.VMEM((SWA_STEP_BLOCKS, SWA_WIDTH, BLOCK), F32)],
        compiler_params=_cparams(("arbitrary", "arbitrary")),
        name="swa",
    )(sinks, rel_bias, qat, kap, kap, vat, vat, bucket_t)


def _fox_kernel(q_ref, k_ref, v_ref, o_ref, m_ref, l_ref, acc_ref, *, tq, tk):
    qi = pl.program_id(1)
    m_ref[...] = jnp.full(m_ref.shape, NEG, F32)
    l_ref[...] = jnp.zeros(l_ref.shape, F32)
    acc_ref[...] = jnp.zeros(acc_ref.shape, F32)
    per_q = tq // tk

    def block(j, masked):
        krow0 = pl.multiple_of(j * tk, tk)
        if masked:
            kpos = j * tk + lax.broadcasted_iota(jnp.int32, (tk, tq), 0)
            qpos = qi * tq + lax.broadcasted_iota(jnp.int32, (tk, tq), 1)
            causal = kpos <= qpos

        qw = tq // FOX_Q_SPLIT
        chains = [(h, c) for h in range(FOX_HEADS) for c in range(FOX_Q_SPLIT)]

        def scores(h, c):
            kblk = k_ref[pl.ds(krow0, tk), h * LANES:(h + 1) * LANES]
            return _dot(kblk, q_ref[h, :, c * qw:(c + 1) * qw])

        def weighted_values(h, c, p, alpha):
            rows, cols = slice(h * HEAD_DIM, (h + 1) * HEAD_DIM), slice(c * qw, (c + 1) * qw)
            pv = _dot(v_ref[j, rows, :], p)
            acc_ref[rows, cols] = alpha * acc_ref[rows, cols] + pv

        ahead = [scores(*chains[i]) for i in range(FOX_AHEAD)]
        pending = None
        for ci, (h, c) in enumerate(chains):
            cols = slice(c * qw, (c + 1) * qw)
            s = ahead.pop(0)
            if ci + FOX_AHEAD < len(chains):
                ahead.append(scores(*chains[ci + FOX_AHEAD]))
            if masked:
                s = jnp.where(causal[:, cols], s, NEG)
            m_old = m_ref[h:h + 1, cols]
            m_new = jnp.maximum(m_old, jnp.max(s, axis=0, keepdims=True))
            alpha = jnp.exp2(m_old - m_new)
            p = jnp.exp2(s - m_new)
            l_ref[h:h + 1, cols] = alpha * l_ref[h:h + 1, cols] + jnp.sum(p, axis=0, keepdims=True)
            m_ref[h:h + 1, cols] = m_new
            if pending is not None:
                weighted_values(*pending)
            pending = (h, c, p.astype(BF16), alpha)
        weighted_values(*pending)

    def body(j, carry):
        block(j, False)
        return carry

    lax.fori_loop(0, qi * per_q, body, 0)
    for jj in range(per_q):
        block(qi * per_q + jj, True)

    outs = []
    for h in range(FOX_HEADS):
        rows = slice(h * HEAD_DIM, (h + 1) * HEAD_DIM)
        outs.append(acc_ref[rows, :] / l_ref[h:h + 1, :])
    o_ref[...] = jnp.concatenate(outs, axis=0).T.astype(o_ref.dtype)


def _fox(qt, kf, vt, bsz, seq, tq):
    tk = FOX_TK
    nq = seq // tq
    hp = kf.shape[1]
    return pl.pallas_call(
        functools.partial(_fox_kernel, tq=tq, tk=tk),
        grid=(bsz, nq),
        in_specs=[pl.BlockSpec((FOX_HEADS, LANES, tq), lambda b, i: (0, 0, b * nq + i)),
                  pl.BlockSpec((seq, hp), lambda b, i: (b, 0)),
                  pl.BlockSpec((seq // tk, FOX_WIDTH, tk), lambda b, i: (b, 0, 0))],
        out_specs=pl.BlockSpec((tq, FOX_WIDTH), lambda b, i: (b * nq + i, 0)),
        out_shape=jax.ShapeDtypeStruct((bsz * seq, FOX_WIDTH), BF16),
        scratch_shapes=[pltpu.VMEM((FOX_HEADS, tq), F32),
                        pltpu.VMEM((FOX_HEADS, tq), F32),
                        pltpu.VMEM((FOX_WIDTH, tq), F32)],
        compiler_params=_cparams(("parallel", "arbitrary")),
        name="fox",
    )(qt, kf, vt)


def _mix_kernel(x_ref, oa_ref, ob_ref, ga_ref, gb_ref, gt1_ref, g2_ref, sc2_ref, sh2_ref,
                wa_ref, wb_ref, wo_ref, wq_ref, x1_ref, h2t_ref, q_ref):
    ya = _dot(oa_ref[...], wa_ref[...])
    yb = _dot(ob_ref[...], wb_ref[...])
    mixed = ga_ref[...].astype(F32) * ya + gb_ref[...].astype(F32) * yb
    x1 = x_ref[...] + gt1_ref[0] * _dot(mixed.astype(BF16), wo_ref[...])
    x1_ref[...] = x1
    h2 = _rms_mod(x1, g2_ref[...], sc2_ref[0], sh2_ref[0])
    h2t_ref[...] = h2.T.astype(BF16)
    q_ref[...] = _dot(h2.astype(BF16), wq_ref[...])


def _mix(x2d, oa, ob, ga, gb, gt1, g2, sc2, sh2, wa, wb, wo, wq, seq, tm):
    t, d = x2d.shape
    per_b = seq // tm
    row = lambda w: pl.BlockSpec((tm, w), lambda i: (i, 0))
    full = lambda a: pl.BlockSpec(a.shape, lambda i: (0,) * a.ndim)
    mod = pl.BlockSpec((1, 1, d), lambda i: (i // per_b, 0, 0))
    return pl.pallas_call(
        _mix_kernel,
        grid=(t // tm,),
        in_specs=[row(d), row(SWA_WIDTH), row(FOX_WIDTH), row(d), row(d), mod, full(g2), mod, mod,
                  full(wa), full(wb), full(wo), full(wq)],
        out_specs=[row(d), pl.BlockSpec((d, tm), lambda i: (0, i)), row(wq.shape[1])],
        out_shape=[jax.ShapeDtypeStruct((t, d), F32), jax.ShapeDtypeStruct((d, t), BF16),
                   jax.ShapeDtypeStruct((t, wq.shape[1]), F32)],
        compiler_params=_cparams(("parallel",)),
        name="mix",
    )(x2d, oa, ob, ga, gb, gt1, g2, sc2, sh2, wa, wb, wo, wq)


def _top_desc(vals, count):
    rowidx = lax.broadcasted_iota(jnp.int32, vals.shape, 0).astype(F32)
    tops = []
    work = vals
    for _ in range(count):
        mx = jnp.max(work, axis=0, keepdims=True)
        first = jnp.min(jnp.where(work == mx, rowidx, float(vals.shape[0])), axis=0, keepdims=True)
        tops.append(mx)
        work = jnp.where(rowidx == first, -jnp.inf, work)
    return tops, work


def _sort_network(n):
    def merge(lo, hi, r):
        step = r * 2
        if step < hi - lo:
            yield from merge(lo, hi, step)
            yield from merge(lo + r, hi, step)
            yield from [(i, i + r) for i in range(lo + r, hi - r, step)]
        else:
            yield (lo, lo + r)

    def sort(lo, hi):
        if hi - lo >= 1:
            mid = lo + (hi - lo) // 2
            yield from sort(lo, mid)
            yield from sort(mid + 1, hi)
            yield from merge(lo, hi, 1)

    return list(sort(0, n - 1))


def _top_desc_sorted(vals, count):
    n = vals.shape[0]
    depth = n // 8
    lists = [vals[8 * r:8 * (r + 1), :] for r in range(depth)]
    for a, b in _sort_network(depth):
        hi, lo = jnp.maximum(lists[a], lists[b]), jnp.minimum(lists[a], lists[b])
        lists[a], lists[b] = hi, lo
    tops = []
    for k in range(count):
        mx = jnp.max(lists[0], axis=0, keepdims=True)
        tops.append(mx)
        need = min(depth, count - 1 - k)
        if need == 0:
            break
        pop = lists[0] == mx
        for r in range(need):
            nxt = lists[r + 1] if r + 1 < depth else -jnp.inf
            lists[r] = jnp.where(pop, nxt, lists[r])
    return tops


def _route_kernel(q_ref, sk_ref, e2_ref, r2_ref, e1_ref, rk_ref):
    q = q_ref[...]
    nk = sk_ref.shape[2]
    sorted_ok = nk % 8 == 0 and (nk // 8) & (nk // 8 - 1) == 0
    top = _top_desc_sorted if sorted_ok else (lambda v, n: _top_desc(v, n)[0])
    k = PEER_TOPK
    for h in range(PEER_HEADS):
        q1 = q[:, (2 * h) * PEER_HALF:(2 * h + 1) * PEER_HALF]
        q2 = q[:, (2 * h + 1) * PEER_HALF:(2 * h + 2) * PEER_HALF]
        s1 = _dot_nt(sk_ref[h, 0], q1)
        s2 = _dot_nt(sk_ref[h, 1], q2)
        a = top(s1, k)
        b = top(s2, k)
        c, n1 = _top_pair_sums(a, b, k)
        z = jnp.zeros_like(c[0])
        for r in range(k):
            z = z + jnp.exp(c[r] - c[0])
        rk = jnp.zeros(s2.shape, F32)
        cnt = jnp.zeros(s1.shape, F32)
        for r in range(k):
            rk = jnp.where(b[r] > s2, float(r + 1), rk)
        for r in reversed(range(k)):
            cnt = jnp.where(s1 == a[r], n1[r], cnt)
        e2_ref[h] = pltpu.bitcast(jnp.exp(s2 - b[0]).astype(BF16), jnp.uint32)
        r2_ref[h] = pltpu.bitcast(rk.astype(BF16), jnp.uint32)
        e1_ref[h] = jnp.exp(s1 - a[0]) * (0.5 / z)
        rk_ref[h] = cnt


def _top_pair_sums(a, b, k):
    assert k == 2 * 8, "two sublane groups of lists"
    t = a[0].shape[1]
    a_lo = jnp.concatenate(a[:8], axis=0)
    a_hi = jnp.concatenate(a[8:], axis=0)
    sub = lax.broadcasted_iota(jnp.int32, (8, t), 0)
    idx_lo = sub.astype(F32)
    idx_hi = idx_lo + 8.0
    neg = jnp.full((8, t), -jnp.inf, F32)
    depth = [a_lo + b[0]] + [jnp.where(sub < k // (d + 1), a_lo + b[d], neg) for d in range(1, k)]
    head_hi = a_hi + b[0]
    n_lo = jnp.zeros((8, t), F32)
    n_hi = jnp.zeros((8, t), F32)
    sums = []
    for step in range(k):
        mx = jnp.max(jnp.maximum(depth[0], head_hi), axis=0, keepdims=True)
        sums.append(mx)
        first = jnp.min(jnp.minimum(jnp.where(depth[0] == mx, idx_lo, float(k)),
                                    jnp.where(head_hi == mx, idx_hi, float(k))), axis=0, keepdims=True)
        pop_lo = idx_lo == first
        pop_hi = idx_hi == first
        n_lo = n_lo + jnp.where(pop_lo, 1.0, 0.0)
        n_hi = n_hi + jnp.where(pop_hi, 1.0, 0.0)
        left = k - 1 - step
        for d in range(min(left, k - 1)):
            nxt = depth[d + 1] if d + 1 < k else neg
            depth[d] = jnp.where(pop_lo, nxt, depth[d])
        head_hi = jnp.where(pop_hi, neg, head_hi)
    n1 = [n_lo[r:r + 1, :] for r in range(8)] + [n_hi[r:r + 1, :] for r in range(8)]
    return sums, n1


def _route(q, sub_keys, tr):
    t, w = q.shape
    nk = sub_keys.shape[2]
    spec = pl.BlockSpec((PEER_HEADS, nk, tr), lambda i: (0, 0, i))
    shp = jax.ShapeDtypeStruct((PEER_HEADS, nk, t), F32)
    spec_j = pl.BlockSpec((PEER_HEADS, nk // 2, tr), lambda i: (0, 0, i))
    shp_j = jax.ShapeDtypeStruct((PEER_HEADS, nk // 2, t), jnp.uint32)
    return pl.pallas_call(
        _route_kernel,
        grid=(t // tr,),
        in_specs=[pl.BlockSpec((tr, w), lambda i: (i, 0)),
                  pl.BlockSpec(sub_keys.shape, lambda i: (0, 0, 0, 0))],
        out_specs=[spec_j, spec_j, spec, spec],
        out_shape=[shp_j, shp_j, shp, shp],
        compiler_params=_cparams(("parallel",)),
        name="route",
    )(q, sub_keys)


def _experts_kernel(h2t_ref, dn_ref, upt_ref, e2_ref, r2_ref, e1_ref, rk_ref, o_ref, act_ref, g_ref,
                    *, nk, ni, tq, ne, n_items):
    s = pl.program_id(0)
    c_item = jnp.clip(s - 2, 0, n_items - 1)

    @pl.when(s == 0)
    def _():
        act_ref[...] = jnp.zeros(act_ref.shape, F32)
        g_ref[...] = jnp.zeros(g_ref.shape, BF16)

    @pl.when(c_item % ne == 0)
    def _():
        o_ref[...] = jnp.zeros(o_ref.shape, F32)

    slot_a = s % 2
    slot_b = (s + 1) % 2
    et = ni * nk
    d = o_ref.shape[0]
    inv_sqrt2 = 2.0 ** -0.5
    mrow = min(MXU_PIECE_ROWS, et, d)
    ncol = min(MXU_PIECE_COLS, tq)

    def piece_a(mr, nc):
        rows, cols = slice(mr * mrow, (mr + 1) * mrow), slice(nc * ncol, (nc + 1) * ncol)
        words = slice(mr * mrow // 2, (mr + 1) * mrow // 2)
        act_ref[slot_a, rows, cols] = _dot(pltpu.bitcast(dn_ref[words, :], BF16), h2t_ref[:, cols])

    def piece_c(mr, nc):
        rows, cols = slice(mr * mrow, (mr + 1) * mrow), slice(nc * ncol, (nc + 1) * ncol)
        words = slice(mr * mrow // 2, (mr + 1) * mrow // 2)
        o_ref[rows, cols] += _dot(pltpu.bitcast(upt_ref[words, :], BF16), g_ref[slot_a, :, cols])

    def block_b(il, c):
        cols = slice(c * LANES, (c + 1) * LANES)
        sub = BF16_SUBLANES
        e1b = [jnp.broadcast_to(e1_ref[h, il:il + 1, cols], (sub, LANES)).astype(BF16) for h in range(PEER_HEADS)]
        rkb = [jnp.broadcast_to(rk_ref[h, il:il + 1, cols], (sub, LANES)).astype(BF16) for h in range(PEER_HEADS)]
        for k in range(nk // sub):
            wr = slice(k * sub // 2, (k + 1) * sub // 2)
            w = None
            for h in range(PEER_HEADS):
                p = pltpu.bitcast(e2_ref[h, wr, cols], BF16) * e1b[h]
                sel = jnp.where(pltpu.bitcast(r2_ref[h, wr, cols], BF16) < rkb[h], p, jnp.zeros_like(p))
                w = sel if w is None else w + sel
            jr = slice(il * nk + k * sub, il * nk + (k + 1) * sub)
            x = act_ref[slot_b, jr, cols]
            g_ref[slot_b, jr, cols] = (x * (1.0 + lax.erf(x * inv_sqrt2))).astype(BF16) * w

    pieces = []
    for nc in range(tq // ncol):
        for mr in range(max(et, d) // mrow):
            if mr < et // mrow:
                pieces.append(functools.partial(piece_a, mr, nc))
            if mr < d // mrow:
                pieces.append(functools.partial(piece_c, mr, nc))
    blocks = [(il, c) for il in range(ni) for c in range(tq // LANES)]
    lead = min(EXPERT_LEAD_PIECES, len(pieces))
    for k in range(lead):
        pieces[k]()
    per_piece = max(1, (len(blocks) // 2) // max(1, len(pieces) - lead + 1))
    nxt = lead
    for k, (il, c) in enumerate(blocks):
        block_b(il, c)
        if (k + 1) % per_piece == 0 and nxt < len(pieces):
            pieces[nxt]()
            nxt += 1
    for k in range(nxt, len(pieces)):
        pieces[k]()


def _experts(h2t, dn, upt, e2, r2, e1, rk, tq, ni):
    d, t = h2t.shape
    n_exp = dn.shape[0] * 2
    nk = e1.shape[1]
    et = ni * nk
    ne = n_exp // et
    n_items = (t // tq) * ne
    a_item = lambda s: jnp.minimum(s, n_items - 1)
    b_item = lambda s: jnp.clip(s - 1, 0, n_items - 1)
    c_item = lambda s: jnp.clip(s - 2, 0, n_items - 1)
    per_j = pl.BlockSpec((PEER_HEADS, nk // 2, tq), lambda s: (0, 0, b_item(s) // ne))
    per_i = pl.BlockSpec((PEER_HEADS, ni, tq), lambda s: (0, b_item(s) % ne, b_item(s) // ne))
    return pl.pallas_call(
        functools.partial(_experts_kernel, nk=nk, ni=ni, tq=tq, ne=ne, n_items=n_items),
        grid=(n_items + 2,),
        in_specs=[pl.BlockSpec((d, tq), lambda s: (0, a_item(s) // ne)),
                  pl.BlockSpec((et // 2, d), lambda s: (a_item(s) % ne, 0)),
                  pl.BlockSpec((d // 2, et), lambda s: (0, c_item(s) % ne)),
                  per_j, per_j, per_i, per_i],
        out_specs=pl.BlockSpec((d, tq), lambda s: (0, c_item(s) // ne)),
        out_shape=jax.ShapeDtypeStruct((d, t), F32),
        scratch_shapes=[pltpu.VMEM((2, et, tq), F32), pltpu.VMEM((2, et, tq), BF16)],
        compiler_params=_cparams(("arbitrary",)),
        name="experts",
    )(h2t, dn, upt, e2, r2, e1, rk)


def _final_kernel(x1_ref, pt_ref, gt2_ref, g_ref, o_ref):
    x2 = x1_ref[...] + gt2_ref[0] * pt_ref[...].T
    ms = jnp.mean(x2 * x2, axis=-1, keepdims=True)
    o_ref[...] = x2 * lax.rsqrt(ms + EPS) * g_ref[...]


def _final(x1, peer_t, gt2, g, seq, tm):
    t, d = x1.shape
    per_b = seq // tm
    return pl.pallas_call(
        _final_kernel,
        grid=(t // tm,),
        in_specs=[pl.BlockSpec((tm, d), lambda i: (i, 0)),
                  pl.BlockSpec((d, tm), lambda i: (0, i)),
                  pl.BlockSpec((1, 1, d), lambda i: (i // per_b, 0, 0)),
                  pl.BlockSpec((1, d), lambda i: (0, 0))],
        out_specs=pl.BlockSpec((tm, d), lambda i: (i, 0)),
        out_shape=jax.ShapeDtypeStruct((t, d), F32),
        compiler_params=_cparams(("parallel",)),
        name="final",
    )(x1, peer_t, gt2, g)


def _t5_bucket(dist):
    dist = np.clip(dist, 0, None)
    max_exact = NUM_BUCKETS // 2
    large = max_exact + (np.log(np.maximum(dist, 1) / max_exact) / np.log(MAX_DISTANCE / max_exact)
                         * (NUM_BUCKETS - max_exact)).astype(np.int32)
    large = np.minimum(large, NUM_BUCKETS - 1)
    return np.where(dist < max_exact, dist, large).astype(np.int32)


def _pack_kernel(x_ref, o_ref, *, transpose):
    x = x_ref[...]
    if transpose:
        x = x.T
    o_ref[...] = pltpu.bitcast(x.astype(BF16), jnp.uint32)


def _pack_bf16_rows(x, transpose=False, tile=512):
    r, c = x.shape
    if transpose:
        in_spec = pl.BlockSpec((tile, c), lambda i: (i, 0))
        out_spec = pl.BlockSpec((c // 2, tile), lambda i: (0, i))
        out_shape = jax.ShapeDtypeStruct((c // 2, r), jnp.uint32)
    else:
        in_spec = pl.BlockSpec((tile, c), lambda i: (i, 0))
        out_spec = pl.BlockSpec((tile // 2, c), lambda i: (i, 0))
        out_shape = jax.ShapeDtypeStruct((r // 2, c), jnp.uint32)
    return pl.pallas_call(
        functools.partial(_pack_kernel, transpose=transpose),
        grid=(r // tile,),
        in_specs=[in_spec],
        out_specs=out_spec,
        out_shape=out_shape,
        compiler_params=_cparams(("parallel",)),
        name="pack_t" if transpose else "pack",
    )(x)


def _row_tile(seq, want):
    tm = min(want, seq)
    assert seq % tm == 0
    return tm


def _fsplit_placement():
    place = np.zeros((N_FSPLIT, LANES, FOX_HEADS * LANES), np.float32)
    for p in range(N_FSPLIT):
        for h in range(FOX_HEADS):
            place[p, h, h * LANES + HEAD_DIM + p] = 1.0
    return jnp.asarray(place, BF16)


def kernel(x, c, norm_mix_g, norm_ffn_g, w_ada, b_ada, w_in, b_forget, sinks, w_branch_a,
           w_branch_b, w_out, rel_bias, w_query, sub_keys, expert_down, expert_up, final_norm_g):
    bsz, seq, d = x.shape
    depth = w_ada.shape[0]
    t = bsz * seq
    tm = _row_tile(seq, 512)
    assert seq % FOX_TK == 0 and tm % FOX_TK == 0

    dist = np.arange(BLOCK)[:, None] + BLOCK - np.arange(2 * BLOCK)[None, :]
    in_band = (dist >= 0) & (dist < WINDOW)
    bucket_t = jnp.asarray(np.where(in_band, _t5_bucket(dist), -1).T, jnp.int32)
    place = _fsplit_placement()

    o_a = SWA_WIDTH + 2 * SWA_KV_WIDTH
    o_q, o_k, o_v = o_a, o_a + FOX_WIDTH, o_a + 2 * FOX_WIDTH
    o_b = o_a + 3 * FOX_WIDTH
    o_f = o_b + FOX_HEADS
    pad_head = LANES - HEAD_DIM

    xc = x.reshape(t, d)
    for l in range(depth):
        mod = _ada(c, w_ada[l], b_ada[l])
        sh1, sc1, gt1, sh2, sc2, gt2 = [m.reshape(bsz, 1, d) for m in jnp.split(mod, N_MOD, axis=-1)]

        w = w_in[l]
        wqat = w[:, :SWA_WIDTH].T.astype(BF16)
        wka3 = w[:, SWA_WIDTH:SWA_WIDTH + SWA_KV_WIDTH].reshape(d, SWA_KV_HEADS, HEAD_DIM)
        wka = jnp.pad(wka3, ((0, 0), (0, 0), (0, pad_head))).reshape(d, SWA_KV_HEADS * LANES).astype(BF16)
        wvat = w[:, SWA_WIDTH + SWA_KV_WIDTH:o_a].T.astype(BF16)
        wq3 = w[:, o_q:o_k].T.reshape(FOX_HEADS, HEAD_DIM, d)
        wqt = jnp.pad(wq3, ((0, 0), (0, pad_head), (0, 0))).reshape(FOX_HEADS * LANES, d).astype(BF16)
        wk3 = w[:, o_k:o_v].reshape(d, FOX_HEADS, HEAD_DIM)
        wk = jnp.pad(wk3, ((0, 0), (0, 0), (0, pad_head))).reshape(d, FOX_HEADS * LANES).astype(BF16)
        wvt = w[:, o_v:o_b].T.astype(BF16)
        wf = jnp.pad(w[:, o_b:o_f], ((0, 0), (0, LANES - FOX_HEADS)))
        wg = w[:, o_f:].astype(BF16)
        qa, ka, va, qt, kpad, vt, fl, ga, gb = _inproj(
            xc, norm_mix_g[l].reshape(1, d), sc1, sh1, wqat, wka, wvat, wqt, wk, wvt, wf, wg, seq, tm)

        bpad = jnp.pad(b_forget[l], (0, LANES - FOX_HEADS)).reshape(1, LANES)
        kf = _kf(fl.reshape(bsz, seq, LANES), bpad, kpad, place)

        oa = _swa(qa, ka, va, sinks[l], rel_bias.astype(F32), bucket_t, bsz, seq)
        ob = _fox(qt, kf, vt, bsz, seq, _row_tile(seq, 512))

        x1, h2t, q = _mix(xc, oa, ob, ga, gb, gt1, norm_ffn_g[l].reshape(1, d), sc2, sh2,
                          w_branch_a[l].astype(BF16), w_branch_b[l].astype(BF16),
                          w_out[l].astype(BF16), w_query[l].astype(BF16), seq, tm)

        e2, r2, e1, rk = _route(q, sub_keys[l], _row_tile(seq, 256))
        ni = 8
        peer_t = _experts(h2t, _pack_bf16_rows(expert_down[l]), _pack_bf16_rows(expert_up[l], transpose=True),
                          e2, r2, e1, rk, _row_tile(seq, EXPERT_TQ), ni)

        g_next = final_norm_g.reshape(1, d)
        xc = _final(x1, peer_t, gt2, g_next, seq, tm)
        assert depth == 1
    return xc.reshape(bsz, seq, d)
```
